```python
import jax
import jax.numpy as jnp
from jax import lax
import numpy as np

D_MODEL = 1024
BATCH = 8
SEQ = 2048
DEPTH = 2
DEC_BATCH = 32
DEC_SEQ = 1
PAST_LEN = 16384
PAGE_SIZE = 128

HEAD_DIM = 64
SB_HEADS = D_MODEL // HEAD_DIM
SB_BIAS_INIT = -6.5
DSW_WINDOWS = (128, 512, 2048)
DSW_DILATIONS = (1, 4, 16)
DSW_GROUPS = 3
DSW_HEADS_PER_GROUP = D_MODEL // HEAD_DIM
DSW_HEADS = DSW_GROUPS * DSW_HEADS_PER_GROUP
DSW_SPAN = DSW_WINDOWS[0] // DSW_DILATIONS[0]
Q_BLOCK = 128
MOE_GROUPS = 4
MOE_EXPERTS_PER_GROUP = 8
MOE_TOP_K = 2
MOE_D_FF = 512
NORM_EPS = 1e-6
N_SB_LAYERS = (DEPTH + 1) // 2
N_DSW_LAYERS = DEPTH // 2

kernel_name = 'hybrid_stickbreak_dilated_hmoe_step'


def rms_norm(x, g):
    xf = x.astype(jnp.float32)
    y = xf * lax.rsqrt(jnp.mean(xf * xf, axis=-1, keepdims=True) + NORM_EPS)
    return (y * g.astype(jnp.float32)).astype(x.dtype)


def stick_breaking_weights(z, mask):
    sp = jnp.where(mask, jax.nn.softplus(z), 0.0)
    after = lax.cumsum(sp, axis=z.ndim - 1, reverse=True) - sp
    return jnp.where(mask, jnp.exp(jax.nn.log_sigmoid(z) - after), 0.0)


def sb_qkv(u, w_in):
    b, s, _ = u.shape
    qkv = (u @ w_in).reshape(b, s, 3, SB_HEADS, HEAD_DIM)
    return qkv[:, :, 0], qkv[:, :, 1], qkv[:, :, 2]


def sb_prompt(u, w_in, bias, w_out):
    b, s, _ = u.shape
    q, k, v = sb_qkv(u, w_in)
    n_blk = s // Q_BLOCK
    q_blocks = q.reshape(b, n_blk, Q_BLOCK, SB_HEADS, HEAD_DIM).swapaxes(0, 1)
    k_pos = jnp.arange(s)
    bias_f = bias.astype(jnp.float32)[:, None, None]

    def one_block(args):
        qb, blk = args
        q_pos = blk * Q_BLOCK + jnp.arange(Q_BLOCK)
        z = jnp.einsum('bqhd,bkhd->bhqk', qb, k, preferred_element_type=jnp.float32) * HEAD_DIM ** -0.5 + bias_f
        w = stick_breaking_weights(z, k_pos[None, :] < q_pos[:, None])
        return jnp.einsum('bhqk,bkhd->bqhd', w.astype(v.dtype), v)

    o = lax.map(one_block, (q_blocks, jnp.arange(n_blk)))
    o = o.swapaxes(0, 1).reshape(b, s, SB_HEADS * HEAD_DIM)
    return o @ w_out, k, v


def sb_sample(u, cache_k, cache_v, page_table, layer_idx, w_in, bias, w_out):
    b, s, _ = u.shape
    q, k, v = sb_qkv(u, w_in)
    past = page_table.shape[1] * PAGE_SIZE
    k_past = cache_k[page_table, layer_idx].reshape(b, past, SB_HEADS, HEAD_DIM)
    v_past = cache_v[page_table, layer_idx].reshape(b, past, SB_HEADS, HEAD_DIM)
    scale = HEAD_DIM ** -0.5
    z = jnp.concatenate([
        jnp.einsum('bqhd,bkhd->bhqk', q, k_past, preferred_element_type=jnp.float32),
        jnp.einsum('bqhd,bkhd->bhqk', q, k, preferred_element_type=jnp.float32)], axis=-1) * scale
    z = z + bias.astype(jnp.float32)[:, None, None]
    new_mask = jnp.arange(s)[None, :] < jnp.arange(s)[:, None]
    mask = jnp.concatenate([jnp.ones((s, past), dtype=bool), new_mask], axis=1)
    w = stick_breaking_weights(z, mask).astype(v.dtype)
    o = (jnp.einsum('bhqk,bkhd->bqhd', w[..., :past], v_past)
         + jnp.einsum('bhqk,bkhd->bqhd', w[..., past:], v))
    return o.reshape(b, s, SB_HEADS * HEAD_DIM) @ w_out, k, v


def alibi_slopes():
    n = jnp.arange(1, DSW_HEADS + 1, dtype=jnp.float32)
    return (2.0 ** (-8.0 * n / DSW_HEADS)).reshape(DSW_GROUPS, DSW_HEADS_PER_GROUP)


def dsw_qkv(u, w_in, gq, gk):
    b, s, _ = u.shape
    qkv = (u @ w_in).reshape(b, s, 3, DSW_GROUPS, DSW_HEADS_PER_GROUP, HEAD_DIM)
    return rms_norm(qkv[:, :, 0], gq), rms_norm(qkv[:, :, 1], gk), qkv[:, :, 2]


def band_attention(q, k, v, step_penalty):
    n, L, h, dh = q.shape
    nb = -(-L // Q_BLOCK)
    pad = nb * Q_BLOCK - L
    qb = jnp.pad(q, ((0, 0), (0, pad), (0, 0), (0, 0))).reshape(n, nb, Q_BLOCK, h, dh)
    kp = jnp.pad(k, ((0, 0), (Q_BLOCK, pad), (0, 0), (0, 0))).reshape(n, nb + 1, Q_BLOCK, h, dh)
    vp = jnp.pad(v, ((0, 0), (Q_BLOCK, pad), (0, 0), (0, 0))).reshape(n, nb + 1, Q_BLOCK, h, dh)
    kc = jnp.concatenate([kp[:, :-1], kp[:, 1:]], axis=2)
    vc = jnp.concatenate([vp[:, :-1], vp[:, 1:]], axis=2)
    c = jnp.arange(2 * Q_BLOCK)
    steps = jnp.arange(Q_BLOCK)[:, None] + Q_BLOCK - c[None, :]
    key_idx = (jnp.arange(nb) * Q_BLOCK - Q_BLOCK)[:, None] + c[None, :]
    valid = ((steps >= 0) & (steps <= DSW_SPAN))[None] & (key_idx >= 0)[:, None, :]
    sc = jnp.einsum('nbqhd,nbkhd->nbhqk', qb, kc, preferred_element_type=jnp.float32) * HEAD_DIM ** -0.5
    sc = sc - step_penalty.astype(jnp.float32)[:, None, None] * steps.astype(jnp.float32)
    sc = jnp.where(valid[None, :, None], sc, -jnp.inf)
    lse = jax.nn.logsumexp(sc, axis=-1)
    p = jnp.exp(sc - lse[..., None])
    o = jnp.einsum('nbhqk,nbkhd->nbqhd', p.astype(v.dtype), vc).reshape(n, nb * Q_BLOCK, h, dh)[:, :L]
    lse = lse.swapaxes(2, 3).reshape(n, nb * Q_BLOCK, h)[:, :L]
    return o, lse


def dilated_band_attention(q, k, v, dil, step_penalty):
    b, s = q.shape[0], q.shape[1]
    L = s // dil

    def to_residue(a):
        return a.reshape(b, L, dil, *a.shape[2:]).swapaxes(1, 2).reshape(b * dil, L, *a.shape[2:])

    def from_residue(a):
        return a.reshape(b, dil, L, *a.shape[2:]).swapaxes(1, 2).reshape(b, s, *a.shape[2:])

    o, lse = band_attention(to_residue(q), to_residue(k), to_residue(v), step_penalty)
    return from_residue(o), from_residue(lse)


def merge_groups(outs, lses):
    alpha = jax.nn.softmax(jnp.stack(lses, axis=0), axis=0)
    return jnp.einsum('gbsh,gbshd->bshd', alpha.astype(outs[0].dtype), jnp.stack(outs, axis=0))


def dsw_prompt(u, w_in, gq, gk, w_out):
    b, s, _ = u.shape
    q, k, v = dsw_qkv(u, w_in, gq, gk)
    slopes = alibi_slopes()
    outs, lses, states = [], [], []
    for g in range(DSW_GROUPS):
        dil = DSW_DILATIONS[g]
        o, lse = dilated_band_attention(q[:, :, g], k[:, :, g], v[:, :, g], dil, slopes[g] * dil)
        outs.append(o)
        lses.append(lse)
        keep = min(DSW_WINDOWS[g], s)
        states.append(jnp.stack([k[:, s - keep:, g], v[:, s - keep:, g]], axis=2))
    o = merge_groups(outs, lses).reshape(b, s, DSW_HEADS_PER_GROUP * HEAD_DIM)
    return o @ w_out, states


def dsw_sample(u, bufs, w_in, gq, gk, w_out):
    b, s, _ = u.shape
    q, k, v = dsw_qkv(u, w_in, gq, gk)
    slopes = alibi_slopes()
    j = jnp.arange(DSW_SPAN + 1)
    outs, lses, states = [], [], []
    for g in range(DSW_GROUPS):
        dil = DSW_DILATIONS[g]
        buf = bufs[g]
        rows = buf.shape[1]
        ck = jnp.concatenate([buf[:, :, 0], k[:, :, g]], axis=1)
        cv = jnp.concatenate([buf[:, :, 1], v[:, :, g]], axis=1)
        idx = rows + jnp.arange(s)[:, None] - dil * j[None, :]
        valid = idx >= 0
        idx = jnp.maximum(idx, 0)
        kg = ck[:, idx]
        vg = cv[:, idx]
        sc = jnp.einsum('bqhd,bqjhd->bhqj', q[:, :, g], kg, preferred_element_type=jnp.float32) * HEAD_DIM ** -0.5
        sc = sc - (slopes[g] * dil)[:, None, None] * j.astype(jnp.float32)
        sc = jnp.where(valid[None, None], sc, -jnp.inf)
        lse = jax.nn.logsumexp(sc, axis=-1)
        p = jnp.exp(sc - lse[..., None])
        outs.append(jnp.einsum('bhqj,bqjhd->bqhd', p.astype(vg.dtype), vg))
        lses.append(lse.swapaxes(1, 2))
        states.append(jnp.stack([ck[:, -rows:], cv[:, -rows:]], axis=2))
    o = merge_groups(outs, lses).reshape(b, s, DSW_HEADS_PER_GROUP * HEAD_DIM)
    return o @ w_out, states


def hier_moe(u, w_group, b_group, w_expert, b_expert, w_gate, w_up, w_down):
    shp = u.shape
    t = u.reshape(-1, shp[-1])
    n_tok = t.shape[0]
    rows = jnp.arange(n_tok)
    g_logits = jnp.dot(t, w_group, preferred_element_type=jnp.float32) + b_group.astype(jnp.float32)
    g_prob = jax.nn.softmax(g_logits, axis=-1)
    g_sel = jnp.argmax(g_logits, axis=-1)
    g_gate = g_prob[rows, g_sel]
    e_logits = jnp.einsum('td,gde->tge', t, w_expert, preferred_element_type=jnp.float32) + b_expert.astype(jnp.float32)
    e_logits = e_logits[rows, g_sel]
    top_val, top_idx = lax.top_k(e_logits, MOE_TOP_K)
    top_w = jax.nn.softmax(top_val, axis=-1) * g_gate[:, None]
    e_gate = jnp.einsum('tk,tke->te', top_w, jax.nn.one_hot(top_idx, MOE_EXPERTS_PER_GROUP, dtype=jnp.float32))
    out = jnp.zeros_like(t)
    for g in range(MOE_GROUPS):
        coef = jnp.where((g_sel == g)[:, None], e_gate, 0.0).astype(t.dtype)
        hg = jax.nn.silu(jnp.einsum('td,edf->tef', t, w_gate[g])) * jnp.einsum('td,edf->tef', t, w_up[g])
        out = out + jnp.einsum('tef,efd->td', hg * coef[:, :, None], w_down[g])
    return out.reshape(shp)


def setup_inputs(seed: int = 0) -> dict:
    key = jax.random.key(seed)
    ks = jax.random.split(key, 24)
    f32 = jnp.float32
    n_pages = PAST_LEN // PAGE_SIZE
    n_used = DEC_BATCH * n_pages
    n_pool = n_used + (n_used + 3) // 4

    def nrm(k, shape, scale=1.0):
        return jax.random.normal(k, shape, f32) * scale

    perm = jax.random.permutation(ks[7], n_pool)
    page_table = perm[:n_used].reshape(DEC_BATCH, n_pages).astype(jnp.int32)
    sb_w = SB_HEADS * HEAD_DIM
    dsw_w = DSW_HEADS_PER_GROUP * HEAD_DIM
    return {
        'x_prompt': nrm(ks[0], (BATCH, SEQ, D_MODEL)),
        'x_sample': nrm(ks[1], (DEC_BATCH, DEC_SEQ, D_MODEL)),
        'cache_sb_k': nrm(ks[2], (n_pool, N_SB_LAYERS, PAGE_SIZE, SB_HEADS, HEAD_DIM)),
        'cache_sb_v': nrm(ks[3], (n_pool, N_SB_LAYERS, PAGE_SIZE, SB_HEADS, HEAD_DIM)),
        'cache_dsw0_kv': nrm(ks[4], (N_DSW_LAYERS, DEC_BATCH, min(DSW_WINDOWS[0], PAST_LEN), 2, DSW_HEADS_PER_GROUP, HEAD_DIM)),
        'cache_dsw1_kv': nrm(ks[5], (N_DSW_LAYERS, DEC_BATCH, min(DSW_WINDOWS[1], PAST_LEN), 2, DSW_HEADS_PER_GROUP, HEAD_DIM)),
        'cache_dsw2_kv': nrm(ks[6], (N_DSW_LAYERS, DEC_BATCH, min(DSW_WINDOWS[2], PAST_LEN), 2, DSW_HEADS_PER_GROUP, HEAD_DIM)),
        'page_table': page_table,
        'norm1_g': 1.0 + nrm(ks[8], (DEPTH, D_MODEL), 0.05),
        'norm2_g': 1.0 + nrm(ks[9], (DEPTH, D_MODEL), 0.05),
        'sb_w_in': nrm(ks[10], (N_SB_LAYERS, D_MODEL, 3 * sb_w), D_MODEL ** -0.5),
        'sb_logit_bias': SB_BIAS_INIT + nrm(ks[23], (N_SB_LAYERS, SB_HEADS), 0.1),
        'sb_w_out': nrm(ks[11], (N_SB_LAYERS, sb_w, D_MODEL), sb_w ** -0.5),
        'dsw_w_in': nrm(ks[12], (N_DSW_LAYERS, D_MODEL, 3 * DSW_HEADS * HEAD_DIM), D_MODEL ** -0.5),
        'dsw_q_norm_g': 1.0 + nrm(ks[13], (N_DSW_LAYERS, HEAD_DIM), 0.05),
        'dsw_k_norm_g': 1.0 + nrm(ks[14], (N_DSW_LAYERS, HEAD_DIM), 0.05),
        'dsw_w_out': nrm(ks[15], (N_DSW_LAYERS, dsw_w, D_MODEL), dsw_w ** -0.5),
        'moe_w_group': nrm(ks[16], (DEPTH, D_MODEL, MOE_GROUPS), D_MODEL ** -0.5),
        'moe_b_group': nrm(ks[17], (DEPTH, MOE_GROUPS), 0.01),
        'moe_w_expert': nrm(ks[18], (DEPTH, MOE_GROUPS, D_MODEL, MOE_EXPERTS_PER_GROUP), D_MODEL ** -0.5),
        'moe_b_expert': nrm(ks[19], (DEPTH, MOE_GROUPS, MOE_EXPERTS_PER_GROUP), 0.01),
        'moe_w_gate': nrm(ks[20], (DEPTH, MOE_GROUPS, MOE_EXPERTS_PER_GROUP, D_MODEL, MOE_D_FF), D_MODEL ** -0.5),
        'moe_w_up': nrm(ks[21], (DEPTH, MOE_GROUPS, MOE_EXPERTS_PER_GROUP, D_MODEL, MOE_D_FF), D_MODEL ** -0.5),
        'moe_w_down': nrm(ks[22], (DEPTH, MOE_GROUPS, MOE_EXPERTS_PER_GROUP, MOE_D_FF, D_MODEL), MOE_D_FF ** -0.5),
    }


def reference(x_prompt, x_sample, cache_sb_k, cache_sb_v, cache_dsw0_kv, cache_dsw1_kv, cache_dsw2_kv,
              page_table, norm1_g, norm2_g, sb_w_in, sb_logit_bias, sb_w_out, dsw_w_in, dsw_q_norm_g,
              dsw_k_norm_g, dsw_w_out, moe_w_group, moe_b_group, moe_w_expert, moe_b_expert, moe_w_gate,
              moe_w_up, moe_w_down):
    dsw_caches = (cache_dsw0_kv, cache_dsw1_kv, cache_dsw2_kv)
    hp, hs = x_prompt, x_sample
    sb_kp, sb_vp, sb_ks, sb_vs = [], [], [], []
    dsw_p = [[] for _ in range(DSW_GROUPS)]
    dsw_s = [[] for _ in range(DSW_GROUPS)]
    for layer in range(DEPTH):
        up = rms_norm(hp, norm1_g[layer])
        us = rms_norm(hs, norm1_g[layer])
        if layer % 2 == 0:
            la = layer // 2
            mp, kp, vp = sb_prompt(up, sb_w_in[la], sb_logit_bias[la], sb_w_out[la])
            ms, kn, vn = sb_sample(us, cache_sb_k, cache_sb_v, page_table, la, sb_w_in[la],
                                   sb_logit_bias[la], sb_w_out[la])
            sb_kp.append(kp)
            sb_vp.append(vp)
            sb_ks.append(kn)
            sb_vs.append(vn)
        else:
            lb = layer // 2
            mp, st_p = dsw_prompt(up, dsw_w_in[lb], dsw_q_norm_g[lb], dsw_k_norm_g[lb], dsw_w_out[lb])
            ms, st_s = dsw_sample(us, [c[lb] for c in dsw_caches], dsw_w_in[lb], dsw_q_norm_g[lb],
                                  dsw_k_norm_g[lb], dsw_w_out[lb])
            for g in range(DSW_GROUPS):
                dsw_p[g].append(st_p[g])
                dsw_s[g].append(st_s[g])
        hp = hp + mp
        hs = hs + ms
        moe_args = (moe_w_group[layer], moe_b_group[layer], moe_w_expert[layer], moe_b_expert[layer],
                    moe_w_gate[layer], moe_w_up[layer], moe_w_down[layer])
        hp = hp + hier_moe(rms_norm(hp, norm2_g[layer]), *moe_args)
        hs = hs + hier_moe(rms_norm(hs, norm2_g[layer]), *moe_args)
    sb_k_prompt = jnp.stack(sb_kp, axis=1)
    sb_v_prompt = jnp.stack(sb_vp, axis=1)
    sb_k_sample = jnp.stack(sb_ks, axis=1)
    sb_v_sample = jnp.stack(sb_vs, axis=1)
    dsw0_kv_prompt = jnp.stack(dsw_p[0], axis=0)
    dsw0_kv_sample = jnp.stack(dsw_s[0], axis=0)
    dsw1_kv_prompt = jnp.stack(dsw_p[1], axis=0)
    dsw1_kv_sample = jnp.stack(dsw_s[1], axis=0)
    dsw2_kv_prompt = jnp.stack(dsw_p[2], axis=0)
    dsw2_kv_sample = jnp.stack(dsw_s[2], axis=0)
    return (hp, hs, sb_k_prompt, sb_v_prompt, sb_k_sample, sb_v_sample,
            dsw0_kv_prompt, dsw0_kv_sample, dsw1_kv_prompt, dsw1_kv_sample, dsw2_kv_prompt, dsw2_kv_sample)
```

```python
import functools

import jax
import jax.numpy as jnp
from jax import lax
from jax.experimental import pallas as pl
from jax.experimental.pallas import tpu as pltpu

F32 = jnp.float32
BF16 = jnp.bfloat16
I32 = jnp.int32

HEAD_DIM = 64
DSW_DILATIONS = (1, 4, 16)
DSW_SPAN = 128
MOE_TOP_K = 2
NORM_EPS = 1e-6
Q_SCALE = HEAD_DIM ** -0.5

LANES = 128
MXU_DIM = 256
VMEM_LIMIT = 48 * 1024 * 1024

NEG_BIG = -1e30

ROW_TILE = 512
COL_TILE = 1024
MOE_TILE = 256
SB_Q_TILE = 256
SB_K_TILE = 256
SB_HEADS_PER_BLOCK = MXU_DIM // HEAD_DIM
SB_PAGES_PER_STEP = 4
DSW_BLOCK = 128
DSW_SAMPLE_CHUNK = 512


def _params(*semantics):
    return pltpu.CompilerParams(dimension_semantics=semantics, vmem_limit_bytes=VMEM_LIMIT)


def _rms(x, g):
    ms = jnp.mean(x * x, axis=-1, keepdims=True)
    return (x * lax.rsqrt(ms + NORM_EPS)) * g


def _dot(a, b):
    return jnp.dot(a, b, preferred_element_type=F32)


def _dot_nt(a, b):
    return lax.dot_general(a, b, (((1,), (1,)), ((), ())), preferred_element_type=F32)


def _qkv_kernel(*refs, nb, head_norm, q_dtype):
    if head_norm:
        x_ref, g_ref, w_ref, gq_ref, gk_ref, bd_ref, q_ref, k_ref, v_ref, xn_ref = refs
    else:
        x_ref, g_ref, w_ref, q_ref, k_ref, v_ref, xn_ref = refs
    j = pl.program_id(1)

    @pl.when(j == 0)
    def _():
        xn_ref[...] = _rms(x_ref[...], g_ref[...]).astype(BF16)

    y = _dot(xn_ref[...], w_ref[...])

    def head_normed(y, gain_ref):
        parts = []
        for c in range(y.shape[1] // MXU_DIM):
            yc = y[:, c * MXU_DIM:(c + 1) * MXU_DIM]
            ms = _dot((yc * yc).astype(BF16), bd_ref[...])
            parts.append(yc * lax.rsqrt(ms + NORM_EPS))
        return jnp.concatenate(parts, axis=1) * gain_ref[...]

    @pl.when(j < nb)
    def _():
        q = head_normed(y, gq_ref) if head_norm else y
        q_ref[...] = (q * Q_SCALE).astype(q_dtype)

    @pl.when((j >= nb) & (j < 2 * nb))
    def _():
        k_ref[...] = head_normed(y, gk_ref) if head_norm else y

    @pl.when(j >= 2 * nb)
    def _():
        v_ref[...] = y


def _qkv_proj(x, g, w_bf16, *, tm, q_dtype, head_gains=None):
    t, d = x.shape
    n = w_bf16.shape[1]
    p = n // 3
    nb = p // COL_TILE
    head_norm = head_gains is not None
    in_specs = [
        pl.BlockSpec((tm, d), lambda i, j: (i, 0)),
        pl.BlockSpec((1, d), lambda i, j: (0, 0)),
        pl.BlockSpec((d, COL_TILE), lambda i, j: (0, j)),
    ]
    args = [x, g.reshape(1, d), w_bf16]
    if head_norm:
        gq, gk = head_gains
        reps = COL_TILE // HEAD_DIM
        head_of = jnp.arange(MXU_DIM) // HEAD_DIM
        bd = jnp.where(head_of[:, None] == head_of[None, :], 1.0 / HEAD_DIM, 0.0).astype(BF16)
        in_specs += [
            pl.BlockSpec((1, COL_TILE), lambda i, j: (0, 0)),
            pl.BlockSpec((1, COL_TILE), lambda i, j: (0, 0)),
            pl.BlockSpec((MXU_DIM, MXU_DIM), lambda i, j: (0, 0)),
        ]
        args += [jnp.tile(gq.astype(F32), reps).reshape(1, COL_TILE),
                 jnp.tile(gk.astype(F32), reps).reshape(1, COL_TILE), bd]

    def part_map(first):
        return lambda i, j: (i, jnp.clip(j - first, 0, nb - 1))

    return pl.pallas_call(
        functools.partial(_qkv_kernel, nb=nb, head_norm=head_norm, q_dtype=q_dtype),
        out_shape=(jax.ShapeDtypeStruct((t, p), q_dtype),
                   jax.ShapeDtypeStruct((t, p), F32),
                   jax.ShapeDtypeStruct((t, p), F32)),
        grid=(t // tm, 3 * nb),
        in_specs=in_specs,
        out_specs=(pl.BlockSpec((tm, COL_TILE), part_map(0)),
                   pl.BlockSpec((tm, COL_TILE), part_map(nb)),
                   pl.BlockSpec((tm, COL_TILE), part_map(2 * nb))),
        scratch_shapes=[pltpu.VMEM((tm, d), BF16)],
        compiler_params=_params("arbitrary", "arbitrary"),
        name="qkv_proj",
    )(*args)


def _qkv_t_kernel(x_ref, g_ref, wt_ref, q_ref, kt_ref, vt_ref, xn_ref):
    j = pl.program_id(1)

    @pl.when(j == 0)
    def _():
        xn_ref[...] = _rms(x_ref[...], g_ref[...]).astype(BF16)
        q_ref[...] = (_dot_nt(xn_ref[...], wt_ref[...]) * Q_SCALE).astype(q_ref.dtype)

    @pl.when(j == 1)
    def _():
        kt_ref[...] = _dot_nt(wt_ref[...], xn_ref[...])

    @pl.when(j == 2)
    def _():
        vt_ref[...] = _dot_nt(wt_ref[...], xn_ref[...])


def _qkv_proj_t(x, g, wt_bf16, *, batch, seq, tm):
    t, d = x.shape
    p = wt_bf16.shape[0] // 3
    tiles_per_seq = seq // tm
    kv_map = lambda i, j: (i // tiles_per_seq, i % tiles_per_seq)
    return pl.pallas_call(
        _qkv_t_kernel,
        out_shape=(jax.ShapeDtypeStruct((t, p), BF16),
                   jax.ShapeDtypeStruct((batch * p, seq), F32),
                   jax.ShapeDtypeStruct((batch * p, seq), F32)),
        grid=(t // tm, 3),
        in_specs=[pl.BlockSpec((tm, d), lambda i, j: (i, 0)),
                  pl.BlockSpec((1, d), lambda i, j: (0, 0)),
                  pl.BlockSpec((p, d), lambda i, j: (j, 0))],
        out_specs=(pl.BlockSpec((tm, p), lambda i, j: (i, 0)),
                   pl.BlockSpec((p, tm), kv_map),
                   pl.BlockSpec((p, tm), kv_map)),
        scratch_shapes=[pltpu.VMEM((tm, d), BF16)],
        compiler_params=_params("arbitrary", "arbitrary"),
        name="qkv_proj_t",
    )(x, g.reshape(1, d), wt_bf16)


def _out_proj_kernel(a_ref, w_ref, res_ref, o_ref):
    o_ref[...] = res_ref[...] + _dot(a_ref[...].astype(BF16), w_ref[...])


def _out_proj(a, w_bf16, res, *, tm):
    t, d_in = a.shape
    d = w_bf16.shape[1]
    return pl.pallas_call(
        _out_proj_kernel,
        out_shape=jax.ShapeDtypeStruct((t, d), F32),
        grid=(t // tm,),
        in_specs=[pl.BlockSpec((tm, d_in), lambda i: (i, 0)),
                  pl.BlockSpec((d_in, d), lambda i: (0, 0)),
                  pl.BlockSpec((tm, d), lambda i: (i, 0))],
        out_specs=pl.BlockSpec((tm, d), lambda i: (i, 0)),
        compiler_params=_params("arbitrary"),
        name="out_proj",
    )(a, w_bf16, res)


def _softplus_parts(z):
    l = jnp.log1p(jnp.exp(-jnp.abs(z)))
    return jnp.maximum(z, 0.0) + l, jnp.minimum(z, 0.0) - l


def _sb_prompt_kernel(bias_ref, q_ref, kt_ref, vt_ref, tri_ref, o_ref, kb_ref, vm_ref, acc_ref, *, tq, tk):
    hq = pl.program_id(1)
    qi = pl.program_id(2)
    nh = SB_HEADS_PER_BLOCK
    s_len = kt_ref.shape[1]
    head_of_lane = lax.broadcasted_iota(I32, (1, MXU_DIM), 1) // HEAD_DIM
    head_of_row = lax.broadcasted_iota(I32, (MXU_DIM, 1), 0) // HEAD_DIM

    @pl.when(qi == 0)
    def _():
        for j in range(s_len // tk):
            cols = slice(j * tk, (j + 1) * tk)
            kb_ref[j] = kt_ref[:, cols].astype(BF16)
            vj = vt_ref[:, cols]
            vm_ref[j] = jnp.concatenate(
                [jnp.where(head_of_row == h, vj, 0.0) for h in range(nh)], axis=1).astype(BF16)

    q = q_ref[...]
    qm = jnp.concatenate([jnp.where(head_of_lane == h, q, jnp.zeros_like(q)) for h in range(nh)], axis=0)
    bias = jnp.concatenate([jnp.full((tq, 1), bias_ref[hq * nh + h], F32) for h in range(nh)], axis=0)
    tri = tri_ref[...]
    acc_ref[...] = jnp.zeros_like(acc_ref)

    def step(j, carry, mask):
        z = _dot(qm, kb_ref[j]) + bias
        sp, ls = _softplus_parts(z)
        if mask is not None:
            sp = jnp.where(mask, sp, 0.0)
        after = _dot(sp.astype(BF16), tri) + carry
        w = jnp.exp(ls - after)
        if mask is not None:
            w = jnp.where(mask, w, 0.0)
        wb = w.astype(BF16)
        wcat = jnp.concatenate([wb[h * tq:(h + 1) * tq] for h in range(nh)], axis=1)
        acc_ref[...] += _dot_nt(wcat, vm_ref[j])
        return carry + jnp.sum(sp, axis=1, keepdims=True)

    row = lax.broadcasted_iota(I32, (nh * tq, tk), 0) % tq
    col = lax.broadcasted_iota(I32, (nh * tq, tk), 1)
    diag = col < row
    carry = step(qi, jnp.zeros((nh * tq, 1), F32), diag)
    lax.fori_loop(0, qi, lambda t, c: step(qi - 1 - t, c, None), carry)
    o_ref[...] = acc_ref[...].astype(o_ref.dtype)


def _sb_prompt_attention(q, kt, vt, bias, *, batch, seq):
    width = q.shape[1]
    blocks_per_batch = width // MXU_DIM
    tq, tk = SB_Q_TILE, SB_K_TILE
    nq = seq // tq
    idx = jnp.arange(tk)
    tri = (idx[:, None] > idx[None, :]).astype(BF16)
    return pl.pallas_call(
        functools.partial(_sb_prompt_kernel, tq=tq, tk=tk),
        out_shape=jax.ShapeDtypeStruct((batch * seq, width), BF16),
        grid=(batch, width // MXU_DIM, nq),
        in_specs=[pl.BlockSpec(memory_space=pltpu.SMEM),
                  pl.BlockSpec((tq, MXU_DIM), lambda b, h, i: (b * nq + i, h)),
                  pl.BlockSpec((MXU_DIM, seq), lambda b, h, i: (b * blocks_per_batch + h, 0)),
                  pl.BlockSpec((MXU_DIM, seq), lambda b, h, i: (b * blocks_per_batch + h, 0)),
                  pl.BlockSpec((tk, tk), lambda b, h, i: (0, 0))],
        out_specs=pl.BlockSpec((tq, MXU_DIM), lambda b, h, i: (b * nq + i, h)),
        scratch_shapes=[pltpu.VMEM((seq // tk, MXU_DIM, tk), BF16),
                        pltpu.VMEM((seq // tk, MXU_DIM, SB_HEADS_PER_BLOCK * tk), BF16),
                        pltpu.VMEM((tq, MXU_DIM), F32)],
        compiler_params=_params("arbitrary", "arbitrary", "arbitrary"),
        name="sb_prompt_attention",
    )(bias.astype(F32), q, kt, vt, tri)


def _head_rows(x_row, n_heads):
    width = x_row.shape[1]
    lane_head = lax.broadcasted_iota(I32, (n_heads, width), 1) // HEAD_DIM
    row = lax.broadcasted_iota(I32, (n_heads, width), 0)
    return jnp.where(lane_head == row, jnp.broadcast_to(x_row, (n_heads, width)), 0.0)


def _head_diag(acc):
    n_heads, width = acc.shape
    lane_head = lax.broadcasted_iota(I32, (n_heads, width), 1) // HEAD_DIM
    row = lax.broadcasted_iota(I32, (n_heads, width), 0)
    return jnp.sum(jnp.where(lane_head == row, acc, 0.0), axis=0, keepdims=True)


def _sb_sample_kernel(pt_ref, q_ref, bias_ref, tri_ref, *refs, n_pages_step, n_heads):
    del pt_ref
    k_refs = refs[:n_pages_step]
    v_refs = refs[n_pages_step:2 * n_pages_step]
    o_ref, acc_ref, carry_ref = refs[2 * n_pages_step:]
    s = pl.program_id(1)

    @pl.when(s == 0)
    def _():
        acc_ref[...] = jnp.zeros_like(acc_ref)
        carry_ref[...] = jnp.zeros_like(carry_ref)

    qh = _head_rows(q_ref[0], n_heads).astype(BF16)
    bias = bias_ref[...]
    tri = tri_ref[...]
    carry = carry_ref[...]
    acc = acc_ref[...]
    for u in range(n_pages_step):
        z = _dot(qh, k_refs[u][...].astype(BF16)) + bias
        sp, ls = _softplus_parts(z)
        after = _dot(sp.astype(BF16), tri) + carry
        w = jnp.exp(ls - after)
        acc = acc + _dot_nt(w.astype(BF16), v_refs[u][...].astype(BF16))
        carry = carry + jnp.sum(sp, axis=1, keepdims=True)
    acc_ref[...] = acc
    carry_ref[...] = carry

    @pl.when(s == pl.num_programs(1) - 1)
    def _():
        o_ref[0] = _head_diag(acc)


def _sb_sample_attention(q, cache_k, cache_v, page_table, bias):
    bd, width = q.shape
    n_heads = width // HEAD_DIM
    page = cache_k.shape[2]
    n_pages = page_table.shape[1]
    pps = SB_PAGES_PER_STEP
    n_steps = n_pages // pps
    idx = jnp.arange(page)
    tri = (idx[:, None] > idx[None, :]).astype(BF16)

    def page_map(u):
        return lambda b, s, pt: (pt[b * n_pages + (n_steps - 1 - s) * pps + (pps - 1 - u)], 0, 0)

    kv_specs = [pl.BlockSpec((None, width, page), page_map(u)) for u in range(pps)]
    out = pl.pallas_call(
        functools.partial(_sb_sample_kernel, n_pages_step=pps, n_heads=n_heads),
        out_shape=jax.ShapeDtypeStruct((bd, 1, width), F32),
        grid_spec=pltpu.PrefetchScalarGridSpec(
            num_scalar_prefetch=1,
            grid=(bd, n_steps),
            in_specs=[pl.BlockSpec((1, 1, width), lambda b, s, pt: (b, 0, 0)),
                      pl.BlockSpec((n_heads, 1), lambda b, s, pt: (0, 0)),
                      pl.BlockSpec((page, page), lambda b, s, pt: (0, 0))] + kv_specs + kv_specs,
            out_specs=pl.BlockSpec((1, 1, width), lambda b, s, pt: (b, 0, 0)),
            scratch_shapes=[pltpu.VMEM((n_heads, width), F32), pltpu.VMEM((n_heads, 1), F32)],
        ),
        compiler_params=_params("arbitrary", "arbitrary"),
        name="sb_sample_attention",
    )(page_table.reshape(-1).astype(I32), q.astype(F32).reshape(bd, 1, width),
      bias.astype(F32).reshape(n_heads, 1), tri, *([cache_k] * pps), *([cache_v] * pps))
    return out.reshape(bd, width)


def _dsw_prompt_kernel(slope_ref, *refs, seq):
    q_refs, k_refs, v_refs = refs[0:3], refs[3:6], refs[6:9]
    o_ref, og_ref, lg_ref = refs[9:]
    hp = pl.program_id(1)
    blk_rows = DSW_BLOCK
    n_groups = len(DSW_DILATIONS)
    heads_per_group = slope_ref.shape[0] // n_groups
    lane = lax.broadcasted_iota(I32, (1, LANES), 1)
    first_head = lane < HEAD_DIM

    qrow = lax.broadcasted_iota(I32, (2 * blk_rows, 2 * blk_rows), 0) % blk_rows
    kcol = lax.broadcasted_iota(I32, (2 * blk_rows, 2 * blk_rows), 1)
    steps = qrow + blk_rows - kcol
    in_band = (steps >= 0) & (steps <= DSW_SPAN)
    steps_f = steps.astype(F32)
    upper = lax.broadcasted_iota(I32, (2 * blk_rows, 1), 0) >= blk_rows

    for g, dil in enumerate(DSW_DILATIONS):
        n_blk = seq // dil // blk_rows
        s0 = slope_ref[g * heads_per_group + 2 * hp]
        s1 = slope_ref[g * heads_per_group + 2 * hp + 1]
        penalty = jnp.where(upper, s1, s0) * steps_f
        q_ref, k_ref, v_ref = q_refs[g], k_refs[g], v_refs[g]

        def rows_at(first, dil=dil):
            if dil == 1:
                return pl.ds(pl.multiple_of(first, blk_rows), blk_rows)
            return pl.ds(first, blk_rows, stride=dil)

        def block(t, _, g=g, dil=dil, n_blk=n_blk, penalty=penalty, q_ref=q_ref, k_ref=k_ref,
                  v_ref=v_ref, rows_at=rows_at):
            res = t // n_blk
            blk = t % n_blk
            own = res + dil * blk * blk_rows
            prev = res + dil * jnp.maximum(blk - 1, 0) * blk_rows
            qb = q_ref[rows_at(own), :]
            kc = jnp.concatenate([k_ref[rows_at(prev), :], k_ref[rows_at(own), :]], axis=0)
            vc = jnp.concatenate([v_ref[rows_at(prev), :], v_ref[rows_at(own), :]], axis=0)
            qm = jnp.concatenate([jnp.where(first_head, qb, 0.0), jnp.where(first_head, 0.0, qb)], axis=0)
            sc = _dot_nt(qm.astype(BF16), kc.astype(BF16)) - penalty
            valid = in_band & ((kcol >= blk_rows) | (blk > 0))
            sc = jnp.where(valid, sc, NEG_BIG)
            m = jnp.max(sc, axis=1, keepdims=True)
            p = jnp.exp(sc - m)
            l = jnp.sum(p, axis=1, keepdims=True)
            pb = p.astype(BF16)
            pcat = jnp.concatenate([pb[:blk_rows], pb[blk_rows:]], axis=1)
            vm = jnp.concatenate([jnp.where(first_head, vc, 0.0), jnp.where(first_head, 0.0, vc)], axis=0)
            o = _dot(pcat, vm.astype(BF16))
            lse = m + jnp.log(l)
            l_b = jnp.where(first_head, l[:blk_rows], l[blk_rows:])
            lse_b = jnp.where(first_head, lse[:blk_rows], lse[blk_rows:])
            og_ref[g, rows_at(own), :] = o / l_b
            lg_ref[g, rows_at(own), :] = lse_b
            return 0

        lax.fori_loop(0, dil * n_blk, block, 0)

    chunk = 2 * blk_rows
    for c in range(seq // chunk):
        rows = pl.ds(c * chunk, chunk)
        ls = [lg_ref[g, rows, :] for g in range(n_groups)]
        m = functools.reduce(jnp.maximum, ls)
        es = [jnp.exp(x - m) for x in ls]
        num = sum(e * og_ref[g, rows, :] for g, e in enumerate(es))
        o_ref[rows, :] = (num / sum(es)).astype(o_ref.dtype)


def _dsw_prompt_attention(q, k, v, slopes, *, batch, seq):
    n_groups = len(DSW_DILATIONS)
    width = q.shape[1] // n_groups
    blocks_per_group = width // LANES

    def col_map(g):
        return lambda b, h: (b, g * blocks_per_group + h)

    specs = [pl.BlockSpec((seq, LANES), col_map(g)) for g in range(n_groups)]
    return pl.pallas_call(
        functools.partial(_dsw_prompt_kernel, seq=seq),
        out_shape=jax.ShapeDtypeStruct((batch * seq, width), BF16),
        grid=(batch, blocks_per_group),
        in_specs=[pl.BlockSpec(memory_space=pltpu.SMEM)] + specs * 3,
        out_specs=pl.BlockSpec((seq, LANES), lambda b, h: (b, h)),
        scratch_shapes=[pltpu.VMEM((n_groups, seq, LANES), F32), pltpu.VMEM((n_groups, seq, LANES), F32)],
        compiler_params=_params("arbitrary", "arbitrary"),
        name="dsw_prompt_attention",
    )(slopes, *([q] * n_groups), *([k] * n_groups), *([v] * n_groups))


def _dsw_sample_kernel(slope_ref, q_ref, kn_ref, vn_ref, *refs, n_heads, rows, chunks):
    n_groups = len(rows)
    buf_refs = refs[:n_groups]
    o_ref, m_ref, den_ref, acc_ref = refs[n_groups:]
    c = pl.program_id(1)
    width = n_heads * HEAD_DIM
    qh = [_head_rows(q_ref[0][:, g * width:(g + 1) * width], n_heads) for g in range(n_groups)]

    @pl.when(c == 0)
    def _():
        sn = [jnp.sum(qh[g] * kn_ref[0][:, g * width:(g + 1) * width], axis=1, keepdims=True)
              for g in range(n_groups)]
        m = functools.reduce(jnp.maximum, sn)
        pn = [jnp.exp(s - m) for s in sn]
        m_ref[...] = m
        den_ref[...] = sum(pn)
        acc_ref[...] = sum(p * vn_ref[0][:, g * width:(g + 1) * width] for g, p in enumerate(pn))

    for g, dil in enumerate(DSW_DILATIONS):
        chunk = chunks[g]

        @pl.when(c < rows[g] // chunk)
        def _(g=g, dil=dil, chunk=chunk):
            kv = buf_refs[g][...]
            row = c * chunk + lax.broadcasted_iota(I32, (1, chunk), 1)
            back = (rows[g] - row).astype(F32)
            sc = _dot(qh[g].astype(BF16), kv[:width].astype(BF16)) - slope_ref[g] * back
            sc = jnp.where((row & (dil - 1)) == 0, sc, NEG_BIG)
            m_old = m_ref[...]
            m_new = jnp.maximum(m_old, jnp.max(sc, axis=1, keepdims=True))
            alpha = jnp.exp(m_old - m_new)
            p = jnp.exp(sc - m_new)
            m_ref[...] = m_new
            den_ref[...] = den_ref[...] * alpha + jnp.sum(p, axis=1, keepdims=True)
            acc_ref[...] = acc_ref[...] * alpha + _dot_nt(p.astype(BF16), kv[width:].astype(BF16))

    @pl.when(c == pl.num_programs(1) - 1)
    def _():
        o_ref[0] = _head_diag(acc_ref[...] / den_ref[...])


def _dsw_sample_attention(q, k_new, v_new, bufs_t, slopes):
    bd, total = q.shape
    n_groups = len(DSW_DILATIONS)
    width = total // n_groups
    n_heads = width // HEAD_DIM
    rows = tuple(b.shape[2] for b in bufs_t)
    for g, dil in enumerate(DSW_DILATIONS):
        assert rows[g] == DSW_SPAN * dil
    chunks = tuple(min(r, DSW_SAMPLE_CHUNK) for r in rows)
    n_steps = max(r // c for r, c in zip(rows, chunks))
    row3 = lambda a: a.astype(F32).reshape(bd, 1, total)

    def buf_spec(g):
        last = rows[g] // chunks[g] - 1
        return pl.BlockSpec((None, 2 * width, chunks[g]), lambda b, c: (b, 0, jnp.minimum(c, last)))

    out = pl.pallas_call(
        functools.partial(_dsw_sample_kernel, n_heads=n_heads, rows=rows, chunks=chunks),
        out_shape=jax.ShapeDtypeStruct((bd, 1, width), F32),
        grid=(bd, n_steps),
        in_specs=[pl.BlockSpec((n_groups, n_heads, 1), lambda b, c: (0, 0, 0))]
        + [pl.BlockSpec((1, 1, total), lambda b, c: (b, 0, 0))] * 3
        + [buf_spec(g) for g in range(n_groups)],
        out_specs=pl.BlockSpec((1, 1, width), lambda b, c: (b, 0, 0)),
        scratch_shapes=[pltpu.VMEM((n_heads, 1), F32), pltpu.VMEM((n_heads, 1), F32),
                        pltpu.VMEM((n_heads, width), F32)],
        compiler_params=_params("arbitrary", "arbitrary"),
        name="dsw_sample_attention",
    )(slopes.reshape(n_groups, n_heads, 1), row3(q), row3(k_new), row3(v_new), *bufs_t)
    return out.reshape(bd, width)


def _router_kernel(hp_ref, hs_ref, g_ref, w_ref, b_ref, xn_ref, eid_ref, gate_ref, *, n_groups, n_experts):
    i = pl.program_id(0)
    n_prompt_tiles = pl.num_programs(0) - 1
    lane = lax.broadcasted_iota(I32, (1, LANES), 1)
    lane_f = lane.astype(F32)
    big = float(LANES)

    def first_lane(hit):
        return jnp.min(jnp.where(hit, lane_f, big), axis=1, keepdims=True).astype(I32)

    def route(x):
        xn = _rms(x, g_ref[...])
        logits = jnp.dot(xn, w_ref[...], preferred_element_type=F32, precision=lax.Precision.HIGHEST) + b_ref[...]
        gl = jnp.where(lane < n_groups, logits, -jnp.inf)
        gmax = jnp.max(gl, axis=1, keepdims=True)
        gsel = first_lane(gl == gmax)
        gden = jnp.sum(jnp.exp(gl - gmax), axis=1, keepdims=True)
        first = n_groups + gsel * n_experts
        el = jnp.where((lane >= first) & (lane < first + n_experts), logits, -jnp.inf)
        m1 = jnp.max(el, axis=1, keepdims=True)
        i1 = first_lane(el == m1)
        el2 = jnp.where(lane == i1, -jnp.inf, el)
        m2 = jnp.max(el2, axis=1, keepdims=True)
        i2 = first_lane(el2 == m2)
        e2 = jnp.exp(m2 - m1)
        w1 = 1.0 / (1.0 + e2) / gden
        w2 = e2 / (1.0 + e2) / gden
        eid = jnp.where(lane == 0, i1 - n_groups, jnp.where(lane == 1, i2 - n_groups, 0))
        gate = jnp.where(lane == 0, w1, jnp.where(lane == 1, w2, 0.0))
        return xn, eid, gate

    @pl.when(i < n_prompt_tiles)
    def _():
        xn, eid, gate = route(hp_ref[...])
        xn_ref[...] = xn
        eid_ref[...] = eid
        gate_ref[...] = gate

    @pl.when(i == n_prompt_tiles)
    def _():
        ns = hs_ref.shape[0]
        xn, eid, gate = route(hs_ref[...])
        xn_ref[...] = jnp.zeros_like(xn_ref)
        eid_ref[...] = jnp.zeros_like(eid_ref)
        gate_ref[...] = jnp.zeros_like(gate_ref)
        xn_ref[0:ns, :] = xn
        eid_ref[0:ns, :] = eid
        gate_ref[0:ns, :] = gate


def _router(h_prompt, h_sample, g, w_router, b_router, *, n_groups, n_experts, tm):
    tp, d = h_prompt.shape
    ns = h_sample.shape[0]
    n_tiles = tp // tm + 1
    last = tp // tm - 1
    return pl.pallas_call(
        functools.partial(_router_kernel, n_groups=n_groups, n_experts=n_experts),
        out_shape=(jax.ShapeDtypeStruct((n_tiles * tm, d), F32),
                   jax.ShapeDtypeStruct((n_tiles * tm, LANES), I32),
                   jax.ShapeDtypeStruct((n_tiles * tm, LANES), F32)),
        grid=(n_tiles,),
        in_specs=[pl.BlockSpec((tm, d), lambda i: (jnp.minimum(i, last), 0)),
                  pl.BlockSpec((ns, d), lambda i: (0, 0)),
                  pl.BlockSpec((1, d), lambda i: (0, 0)),
                  pl.BlockSpec((d, LANES), lambda i: (0, 0)),
                  pl.BlockSpec((1, LANES), lambda i: (0, 0))],
        out_specs=(pl.BlockSpec((tm, d), lambda i: (i, 0)),
                   pl.BlockSpec((tm, LANES), lambda i: (i, 0)),
                   pl.BlockSpec((tm, LANES), lambda i: (i, 0))),
        compiler_params=_params("arbitrary"),
        name="moe_router",
    )(h_prompt, h_sample, g.reshape(1, d), w_router, b_router)


def _moe_kernel(tile_e_ref, nvalid_ref, rows_ref, tok_ref, tok_next_ref, dst_ref, gate_ref, x_hbm,
                wg_ref, wu_ref, wd_ref, y_hbm, xbuf, ybuf, wgb, wub, wdb, gsem, ssem):
    i = pl.program_id(0)
    nv = nvalid_ref[0]
    _, tm, d = xbuf.shape
    slot = lax.rem(i, 2)

    def gather(idx_ref, s):
        def issue(r, _):
            pltpu.make_async_copy(x_hbm.at[pl.ds(idx_ref[0, 0, r], 1), :], xbuf.at[s, pl.ds(r, 1), :],
                                  gsem.at[s]).start()
            return 0
        lax.fori_loop(0, tm, issue, 0, unroll=8)

    def gather_done(s):
        return pltpu.make_async_copy(x_hbm.at[pl.ds(0, tm), :], xbuf.at[s], gsem.at[s])

    def scatter_wait(s, tile):
        def rows_done(n):
            return pltpu.make_async_copy(ybuf.at[s, pl.ds(0, n), :], y_hbm.at[pl.ds(0, n), pl.ds(0, d)], ssem.at[s])
        n = rows_ref[tile]
        n8 = pl.multiple_of((n >> 3) << 3, 8)

        @pl.when(n8 > 0)
        def _():
            rows_done(n8).wait()

        def one_row(_, c):
            rows_done(1).wait()
            return c
        lax.fori_loop(0, n & 7, one_row, 0)

    @pl.when(i == 0)
    def _():
        gather(tok_ref, 0)

    @pl.when(i < nv)
    def _():
        gather_done(slot).wait()

        @pl.when(i + 1 < nv)
        def _():
            gather(tok_next_ref, 1 - slot)

        @pl.when((i == 0) | (tile_e_ref[i] != tile_e_ref[jnp.maximum(i - 1, 0)]))
        def _():
            wgb[...] = wg_ref[...].astype(BF16)
            wub[...] = wu_ref[...].astype(BF16)
            wdb[...] = wd_ref[...].astype(BF16)

        x = xbuf[slot].astype(BF16)
        hg = _dot(x, wgb[...])
        hu = _dot(x, wub[...])
        h = (hg / (1.0 + jnp.exp(-hg))) * hu * gate_ref[...]
        y = _dot(h.astype(BF16), wdb[...])

        @pl.when(i >= 2)
        def _():
            scatter_wait(slot, i - 2)

        ybuf[slot] = y
        n_rows = rows_ref[i]

        def issue(r, _):
            @pl.when(r < n_rows)
            def _():
                dst = dst_ref[0, 0, r]
                col = pl.multiple_of((dst & 1) * d, d)
                pltpu.make_async_copy(ybuf.at[slot, pl.ds(r, 1), :],
                                      y_hbm.at[pl.ds(dst >> 1, 1), pl.ds(col, d)], ssem.at[slot]).start()
            return 0
        lax.fori_loop(0, tm, issue, 0, unroll=8)

        @pl.when(i == nv - 1)
        def _():
            scatter_wait(slot, i)

            @pl.when(i >= 1)
            def _():
                scatter_wait(1 - slot, i - 1)


def _moe_experts(xn_all, eid, gate, w_gate, w_up, w_down, *, n_real):
    d = xn_all.shape[1]
    n_exp, _, d_ff = w_gate.shape
    tm = MOE_TILE
    n_slots = n_real * MOE_TOP_K
    n_tiles = -(-n_slots // tm) + n_exp
    n_pos = n_tiles * tm

    e_flat = eid.reshape(-1)
    tok_flat = jnp.repeat(jnp.arange(n_real, dtype=I32), MOE_TOP_K)
    k_flat = jnp.tile(jnp.arange(MOE_TOP_K, dtype=I32), n_real)
    onehot = (e_flat[:, None] == jnp.arange(n_exp, dtype=I32)[None, :]).astype(I32)
    csum = jnp.cumsum(onehot, axis=0)
    rank = jnp.sum((csum - 1) * onehot, axis=1)
    counts = csum[-1]
    tiles_e = (counts + tm - 1) // tm
    tiles_end = jnp.cumsum(tiles_e)
    tile_start = tiles_end - tiles_e
    pos = jnp.sum(onehot * tile_start[None, :], axis=1) * tm + rank
    n_valid = tiles_end[-1]
    tile_ids = jnp.arange(n_tiles, dtype=I32)
    tile_e = jnp.minimum(jnp.sum((tiles_end[None, :] <= tile_ids[:, None]).astype(I32), axis=1), n_exp - 1)
    tile_first = jnp.sum((tile_e[:, None] == jnp.arange(n_exp, dtype=I32)[None, :]) * tile_start[None, :], axis=1)
    tile_count = jnp.sum((tile_e[:, None] == jnp.arange(n_exp, dtype=I32)[None, :]) * counts[None, :], axis=1)
    tile_rows = jnp.where(tile_ids < n_valid, jnp.clip(tile_count - (tile_ids - tile_first) * tm, 0, tm), 0).astype(I32)
    tile_e = jnp.where(tile_ids < n_valid, tile_e, tile_e[jnp.maximum(n_valid - 1, 0)]).astype(I32)
    tok_tbl = jnp.zeros((n_pos,), I32).at[pos].set(tok_flat, unique_indices=True)
    dst_tbl = jnp.zeros((n_pos,), I32).at[pos].set(tok_flat * MOE_TOP_K + k_flat, unique_indices=True)
    gate_tbl = jnp.zeros((n_pos,), F32).at[pos].set(gate.reshape(-1), unique_indices=True)
    tok3 = tok_tbl.reshape(n_tiles, 1, tm)
    dst3 = dst_tbl.reshape(n_tiles, 1, tm)

    smem_tile = lambda off: pl.BlockSpec(
        (1, 1, tm), lambda i, te, nv, tr: (jnp.minimum(i + off, n_tiles - 1), 0, 0), memory_space=pltpu.SMEM)
    expert_block = lambda shape: pl.BlockSpec((None,) + shape, lambda i, te, nv, tr: (te[i], 0, 0))
    return pl.pallas_call(
        _moe_kernel,
        out_shape=jax.ShapeDtypeStruct((n_real, MOE_TOP_K * d), F32),
        grid_spec=pltpu.PrefetchScalarGridSpec(
            num_scalar_prefetch=3,
            grid=(n_tiles,),
            in_specs=[smem_tile(0), smem_tile(1), smem_tile(0),
                      pl.BlockSpec((tm, 1), lambda i, te, nv, tr: (i, 0)),
                      pl.BlockSpec(memory_space=pl.ANY),
                      expert_block((d, d_ff)), expert_block((d, d_ff)), expert_block((d_ff, d))],
            out_specs=pl.BlockSpec(memory_space=pl.ANY),
            scratch_shapes=[pltpu.VMEM((2, tm, d), F32), pltpu.VMEM((2, tm, d), F32),
                            pltpu.VMEM((d, d_ff), BF16), pltpu.VMEM((d, d_ff), BF16), pltpu.VMEM((d_ff, d), BF16),
                            pltpu.SemaphoreType.DMA((2,)), pltpu.SemaphoreType.DMA((2,))],
        ),
        compiler_params=_params("arbitrary"),
        name="moe_experts",
    )(tile_e, n_valid.reshape(1).astype(I32), tile_rows, tok3, tok3, dst3, gate_tbl.reshape(n_pos, 1), xn_all,
      w_gate, w_up, w_down)


def _combine_kernel(h_ref, y_ref, o_ref):
    d = h_ref.shape[1]
    o_ref[...] = h_ref[...] + (y_ref[:, :d] + y_ref[:, d:])


def _combine(h, y_pairs, *, first_row, tm):
    t, d = h.shape
    b0 = first_row // tm
    return pl.pallas_call(
        _combine_kernel,
        out_shape=jax.ShapeDtypeStruct((t, d), F32),
        grid=(t // tm,),
        in_specs=[pl.BlockSpec((tm, d), lambda i: (i, 0)),
                  pl.BlockSpec((tm, MOE_TOP_K * d), lambda i: (b0 + i, 0))],
        out_specs=pl.BlockSpec((tm, d), lambda i: (i, 0)),
        compiler_params=_params("arbitrary"),
        name="moe_combine",
    )(h, y_pairs)


def _hier_moe(hp, hs, g, w_group, b_group, w_expert, b_expert, w_gate, w_up, w_down):
    tp, d = hp.shape
    ns = hs.shape[0]
    n_groups, _, n_experts = w_expert.shape
    n_route = n_groups * (1 + n_experts)
    w_router = jnp.concatenate(
        [w_group, jnp.moveaxis(w_expert, 0, 1).reshape(d, n_groups * n_experts),
         jnp.zeros((d, LANES - n_route), F32)], axis=1).astype(F32)
    b_router = jnp.concatenate(
        [b_group, b_expert.reshape(-1), jnp.zeros((LANES - n_route,), F32)]).astype(F32).reshape(1, LANES)
    xn_all, eid, gate = _router(hp, hs, g, w_router, b_router, n_groups=n_groups, n_experts=n_experts, tm=ROW_TILE)
    n_real = tp + ns
    y_pairs = _moe_experts(xn_all, eid[:n_real, :MOE_TOP_K], gate[:n_real, :MOE_TOP_K],
                           w_gate.reshape(n_groups * n_experts, d, -1), w_up.reshape(n_groups * n_experts, d, -1),
                           w_down.reshape(n_groups * n_experts, -1, d), n_real=n_real)
    return (_combine(hp, y_pairs, first_row=0, tm=ROW_TILE),
            _combine(hs, y_pairs, first_row=tp, tm=ns))


def kernel(x_prompt, x_sample, cache_sb_k, cache_sb_v, cache_dsw0_kv, cache_dsw1_kv, cache_dsw2_kv, page_table, norm1_g, norm2_g, sb_w_in, sb_logit_bias, sb_w_out, dsw_w_in, dsw_q_norm_g, dsw_k_norm_g, dsw_w_out, moe_w_group, moe_b_group, moe_w_expert, moe_b_expert, moe_w_gate, moe_w_up, moe_w_down):
    batch, seq, d = x_prompt.shape
    bd, dec_seq, _ = x_sample.shape
    assert dec_seq == 1
    depth = norm1_g.shape[0]
    dsw_caches = (cache_dsw0_kv, cache_dsw1_kv, cache_dsw2_kv)
    n_groups = len(DSW_DILATIONS)
    sb_heads = cache_sb_k.shape[3]
    dsw_heads = cache_dsw0_kv.shape[4]
    page = cache_sb_k.shape[2]
    tp = batch * seq

    hp = x_prompt.reshape(tp, d)
    hs = x_sample.reshape(bd, d)
    sb_kp, sb_vp, sb_ks, sb_vs = [], [], [], []
    dsw_p = [[] for _ in range(n_groups)]
    dsw_s = [[] for _ in range(n_groups)]

    for layer in range(depth):
        if layer % 2 == 0:
            la = layer // 2
            w_in = sb_w_in[la].astype(BF16)
            w_out = sb_w_out[la].astype(BF16)
            width = sb_heads * HEAD_DIM
            qp, kpt, vpt = _qkv_proj_t(hp, norm1_g[layer], w_in.T, batch=batch, seq=seq, tm=ROW_TILE)
            qs, ks, vs = _qkv_proj(hs, norm1_g[layer], w_in, tm=bd, q_dtype=F32)
            op = _sb_prompt_attention(qp, kpt, vpt, sb_logit_bias[la], batch=batch, seq=seq)
            pages_t = lambda c: jnp.transpose(c[:, la], (0, 2, 3, 1)).reshape(-1, width, page)
            os_ = _sb_sample_attention(qs, pages_t(cache_sb_k), pages_t(cache_sb_v), page_table, sb_logit_bias[la])
            hp = _out_proj(op, w_out, hp, tm=ROW_TILE)
            hs = _out_proj(os_, w_out, hs, tm=bd)
            seq_major = lambda a: jnp.transpose(a.reshape(batch, sb_heads, HEAD_DIM, seq), (0, 3, 1, 2))
            sb_kp.append(seq_major(kpt))
            sb_vp.append(seq_major(vpt))
            sb_ks.append(ks.reshape(bd, 1, sb_heads, HEAD_DIM))
            sb_vs.append(vs.reshape(bd, 1, sb_heads, HEAD_DIM))
        else:
            lb = layer // 2
            w_in = dsw_w_in[lb].astype(BF16)
            w_out = dsw_w_out[lb].astype(BF16)
            gains = (dsw_q_norm_g[lb], dsw_k_norm_g[lb])
            width = dsw_heads * HEAD_DIM
            n = jnp.arange(1, n_groups * dsw_heads + 1, dtype=F32)
            slopes = (2.0 ** (-8.0 * n / (n_groups * dsw_heads))).reshape(n_groups, dsw_heads)
            step_penalty = slopes * jnp.asarray(DSW_DILATIONS, F32)[:, None]
            qp, kp, vp = _qkv_proj(hp, norm1_g[layer], w_in, tm=ROW_TILE, q_dtype=F32, head_gains=gains)
            qs, ks, vs = _qkv_proj(hs, norm1_g[layer], w_in, tm=bd, q_dtype=F32, head_gains=gains)
            op = _dsw_prompt_attention(qp, kp, vp, step_penalty.reshape(-1), batch=batch, seq=seq)
            bufs_t = [jnp.transpose(c[lb], (0, 2, 3, 4, 1)).reshape(bd, 2 * width, c.shape[2]) for c in dsw_caches]
            os_ = _dsw_sample_attention(qs, ks, vs, bufs_t, slopes)
            hp = _out_proj(op, w_out, hp, tm=ROW_TILE)
            hs = _out_proj(os_, w_out, hs, tm=bd)
            kp4 = kp.reshape(batch, seq, n_groups, dsw_heads, HEAD_DIM)
            vp4 = vp.reshape(batch, seq, n_groups, dsw_heads, HEAD_DIM)
            ks4 = ks.reshape(bd, 1, n_groups, dsw_heads, HEAD_DIM)
            vs4 = vs.reshape(bd, 1, n_groups, dsw_heads, HEAD_DIM)
            for g in range(n_groups):
                keep = min(dsw_caches[g].shape[2], seq)
                dsw_p[g].append(jnp.stack([kp4[:, seq - keep:, g], vp4[:, seq - keep:, g]], axis=2))
                new_row = jnp.stack([ks4[:, :, g], vs4[:, :, g]], axis=2)
                dsw_s[g].append(jnp.concatenate([dsw_caches[g][lb][:, 1:], new_row], axis=1))
        hp, hs = _hier_moe(hp, hs, norm2_g[layer], moe_w_group[layer], moe_b_group[layer], moe_w_expert[layer],
                           moe_b_expert[layer], moe_w_gate[layer], moe_w_up[layer], moe_w_down[layer])

    return (hp.reshape(batch, seq, d), hs.reshape(bd, 1, d),
            jnp.stack(sb_kp, axis=1), jnp.stack(sb_vp, axis=1),
            jnp.stack(sb_ks, axis=1), jnp.stack(sb_vs, axis=1),
            jnp.stack(dsw_p[0], axis=0), jnp.stack(dsw_s[0], axis=0),
            jnp.stack(dsw_p[1], axis=0), jnp.stack(dsw_s[1], axis=0),
            jnp.stack(dsw_p[2], axis=0), jnp.stack(dsw_s[2], axis=0))
```

```python
import functools

import jax
import jax.numpy as jnp
from jax import lax
from jax.experimental import pallas as pl
from jax.experimental.pallas import tpu as pltpu

F32 = jnp.float32
BF16 = jnp.bfloat16
I32 = jnp.int32

HEAD_DIM = 64
DSW_DILATIONS = (1, 4, 16)
DSW_SPAN = 128
MOE_TOP_K = 2
NORM_EPS = 1e-6
Q_SCALE = HEAD_DIM ** -0.5

LANES = 128
TOKEN_ROWS = 8
MXU_DIM = 256
VMEM_LIMIT = 48 * 1024 * 1024

NEG_BIG = -1e30

ROW_TILE = 512
COL_TILE = 1024
MOE_TILE = 256
SB_Q_TILE = 256
SB_K_TILE = 256
SB_HEADS_PER_BLOCK = MXU_DIM // HEAD_DIM
SB_PAGES_PER_STEP = 4
DSW_BLOCK = 128
DSW_BLOCKS_PER_ITER = 8
DSW_SAMPLE_HEADS = MXU_DIM // HEAD_DIM


def _params(*semantics):
    return pltpu.CompilerParams(dimension_semantics=semantics, vmem_limit_bytes=VMEM_LIMIT)


def _rms(x, g):
    ms = jnp.mean(x * x, axis=-1, keepdims=True)
    return (x * lax.rsqrt(ms + NORM_EPS)) * g


def _dot(a, b):
    return jnp.dot(a, b, preferred_element_type=F32)


def _dot_nt(a, b):
    return lax.dot_general(a, b, (((1,), (1,)), ((), ())), preferred_element_type=F32)


def _qkv_kernel(*refs, nb, head_norm, q_dtype):
    if head_norm:
        x_ref, g_ref, w_ref, gq_ref, gk_ref, bd_ref, q_ref, k_ref, v_ref, xn_ref = refs
    else:
        x_ref, g_ref, w_ref, q_ref, k_ref, v_ref, xn_ref = refs
    j = pl.program_id(1)

    @pl.when(j == 0)
    def _():
        xn_ref[...] = _rms(x_ref[...], g_ref[...]).astype(BF16)

    y = _dot(xn_ref[...], w_ref[...])

    def head_normed(y, gain_ref):
        parts = []
        for c in range(y.shape[1] // MXU_DIM):
            yc = y[:, c * MXU_DIM:(c + 1) * MXU_DIM]
            sq = yc * yc
            hi = sq.astype(BF16)
            lo = (sq - hi.astype(F32)).astype(BF16)
            ms = _dot(hi, bd_ref[...]) + _dot(lo, bd_ref[...])
            parts.append(yc * lax.rsqrt(ms + NORM_EPS))
        return jnp.concatenate(parts, axis=1) * gain_ref[...]

    @pl.when(j < nb)
    def _():
        q = head_normed(y, gq_ref) if head_norm else y
        q_ref[...] = (q * Q_SCALE).astype(q_dtype)

    @pl.when((j >= nb) & (j < 2 * nb))
    def _():
        k_ref[...] = head_normed(y, gk_ref) if head_norm else y

    @pl.when(j >= 2 * nb)
    def _():
        v_ref[...] = y


def _qkv_proj(x, g, w_bf16, *, tm, q_dtype, head_gains=None):
    t, d = x.shape
    n = w_bf16.shape[1]
    p = n // 3
    nb = p // COL_TILE
    head_norm = head_gains is not None
    in_specs = [
        pl.BlockSpec((tm, d), lambda i, j: (i, 0)),
        pl.BlockSpec((1, d), lambda i, j: (0, 0)),
        pl.BlockSpec((d, COL_TILE), lambda i, j: (0, j)),
    ]
    args = [x, g.reshape(1, d), w_bf16]
    if head_norm:
        gq, gk = head_gains
        reps = COL_TILE // HEAD_DIM
        head_of = jnp.arange(MXU_DIM) // HEAD_DIM
        bd = jnp.where(head_of[:, None] == head_of[None, :], 1.0 / HEAD_DIM, 0.0).astype(BF16)
        in_specs += [
            pl.BlockSpec((1, COL_TILE), lambda i, j: (0, 0)),
            pl.BlockSpec((1, COL_TILE), lambda i, j: (0, 0)),
            pl.BlockSpec((MXU_DIM, MXU_DIM), lambda i, j: (0, 0)),
        ]
        args += [jnp.tile(gq.astype(F32), reps).reshape(1, COL_TILE),
                 jnp.tile(gk.astype(F32), reps).reshape(1, COL_TILE), bd]

    def part_map(first):
        return lambda i, j: (i, jnp.clip(j - first, 0, nb - 1))

    return pl.pallas_call(
        functools.partial(_qkv_kernel, nb=nb, head_norm=head_norm, q_dtype=q_dtype),
        out_shape=(jax.ShapeDtypeStruct((t, p), q_dtype),
                   jax.ShapeDtypeStruct((t, p), F32),
                   jax.ShapeDtypeStruct((t, p), F32)),
        grid=(t // tm, 3 * nb),
        in_specs=in_specs,
        out_specs=(pl.BlockSpec((tm, COL_TILE), part_map(0)),
                   pl.BlockSpec((tm, COL_TILE), part_map(nb)),
                   pl.BlockSpec((tm, COL_TILE), part_map(2 * nb))),
        scratch_shapes=[pltpu.VMEM((tm, d), BF16)],
        compiler_params=_params("arbitrary", "arbitrary"),
        name="qkv_proj",
    )(*args)


def _qkv_t_kernel(x_ref, g_ref, wt_ref, q_ref, kt_ref, vt_ref, xn_ref):
    j = pl.program_id(1)

    @pl.when(j == 0)
    def _():
        xn_ref[...] = _rms(x_ref[...], g_ref[...]).astype(BF16)
        q_ref[...] = (_dot_nt(xn_ref[...], wt_ref[...]) * Q_SCALE).astype(q_ref.dtype)

    @pl.when(j == 1)
    def _():
        kt_ref[...] = _dot_nt(wt_ref[...], xn_ref[...])

    @pl.when(j == 2)
    def _():
        vt_ref[...] = _dot_nt(wt_ref[...], xn_ref[...])


def _qkv_proj_t(x, g, wt_bf16, *, batch, seq, tm):
    t, d = x.shape
    p = wt_bf16.shape[0] // 3
    tiles_per_seq = seq // tm
    kv_map = lambda i, j: (i // tiles_per_seq, i % tiles_per_seq)
    return pl.pallas_call(
        _qkv_t_kernel,
        out_shape=(jax.ShapeDtypeStruct((t, p), BF16),
                   jax.ShapeDtypeStruct((batch * p, seq), F32),
                   jax.ShapeDtypeStruct((batch * p, seq), F32)),
        grid=(t // tm, 3),
        in_specs=[pl.BlockSpec((tm, d), lambda i, j: (i, 0)),
                  pl.BlockSpec((1, d), lambda i, j: (0, 0)),
                  pl.BlockSpec((p, d), lambda i, j: (j, 0))],
        out_specs=(pl.BlockSpec((tm, p), lambda i, j: (i, 0)),
                   pl.BlockSpec((p, tm), kv_map),
                   pl.BlockSpec((p, tm), kv_map)),
        scratch_shapes=[pltpu.VMEM((tm, d), BF16)],
        compiler_params=_params("arbitrary", "arbitrary"),
        name="qkv_proj_t",
    )(x, g.reshape(1, d), wt_bf16)


def _out_proj_kernel(a_ref, w_ref, res_ref, o_ref):
    o_ref[...] = res_ref[...] + _dot(a_ref[...].astype(BF16), w_ref[...])


def _out_proj(a, w_bf16, res, *, tm):
    t, d_in = a.shape
    d = w_bf16.shape[1]
    return pl.pallas_call(
        _out_proj_kernel,
        out_shape=jax.ShapeDtypeStruct((t, d), F32),
        grid=(t // tm,),
        in_specs=[pl.BlockSpec((tm, d_in), lambda i: (i, 0)),
                  pl.BlockSpec((d_in, d), lambda i: (0, 0)),
                  pl.BlockSpec((tm, d), lambda i: (i, 0))],
        out_specs=pl.BlockSpec((tm, d), lambda i: (i, 0)),
        compiler_params=_params("arbitrary"),
        name="out_proj",
    )(a, w_bf16, res)


def _softplus_parts(z):
    sp = jnp.maximum(z, 0.0) + jnp.log(1.0 + jnp.exp(-jnp.abs(z)))
    return sp, z - sp


def _sb_prompt_kernel(bias_ref, q_ref, kt_ref, vt_ref, tri_ref, o_ref, kb_ref, vm_ref, acc_ref, *, tq, tk):
    hq = pl.program_id(1)
    qi = pl.program_id(2)
    nh = SB_HEADS_PER_BLOCK
    s_len = kt_ref.shape[1]
    head_of_lane = lax.broadcasted_iota(I32, (1, MXU_DIM), 1) // HEAD_DIM
    head_of_row = lax.broadcasted_iota(I32, (MXU_DIM, 1), 0) // HEAD_DIM

    @pl.when(qi == 0)
    def _():
        for j in range(s_len // tk):
            cols = slice(j * tk, (j + 1) * tk)
            kb_ref[j] = kt_ref[:, cols].astype(BF16)
            vj = vt_ref[:, cols]
            vm_ref[j] = jnp.concatenate(
                [jnp.where(head_of_row == h, vj, 0.0) for h in range(nh)], axis=1).astype(BF16)

    q = q_ref[...]
    qm = jnp.concatenate([jnp.where(head_of_lane == h, q, jnp.zeros_like(q)) for h in range(nh)], axis=0)
    bias = jnp.concatenate([jnp.full((tq, 1), bias_ref[hq * nh + h], F32) for h in range(nh)], axis=0)
    tri = tri_ref[...]
    acc_ref[...] = jnp.zeros_like(acc_ref)

    def step(j, carry, mask):
        z = _dot(qm, kb_ref[j]) + bias
        sp, ls = _softplus_parts(z)
        if mask is not None:
            sp = jnp.where(mask, sp, 0.0)
        after = _dot(sp.astype(BF16), tri) + carry
        w = jnp.exp(ls - after)
        if mask is not None:
            w = jnp.where(mask, w, 0.0)
        wb = w.astype(BF16)
        wcat = jnp.concatenate([wb[h * tq:(h + 1) * tq] for h in range(nh)], axis=1)
        acc_ref[...] += _dot_nt(wcat, vm_ref[j])
        return carry + jnp.sum(sp, axis=1, keepdims=True)

    row = lax.broadcasted_iota(I32, (nh * tq, tk), 0) % tq
    col = lax.broadcasted_iota(I32, (nh * tq, tk), 1)
    diag = col < row
    carry = step(qi, jnp.zeros((nh * tq, 1), F32), diag)
    lax.fori_loop(0, qi, lambda t, c: step(qi - 1 - t, c, None), carry)
    o_ref[...] = acc_ref[...].astype(o_ref.dtype)


def _sb_prompt_attention(q, kt, vt, bias, *, batch, seq):
    width = q.shape[1]
    blocks_per_batch = width // MXU_DIM
    tq, tk = SB_Q_TILE, SB_K_TILE
    nq = seq // tq
    idx = jnp.arange(tk)
    tri = (idx[:, None] > idx[None, :]).astype(BF16)
    return pl.pallas_call(
        functools.partial(_sb_prompt_kernel, tq=tq, tk=tk),
        out_shape=jax.ShapeDtypeStruct((batch * seq, width), BF16),
        grid=(batch, width // MXU_DIM, nq),
        in_specs=[pl.BlockSpec(memory_space=pltpu.SMEM),
                  pl.BlockSpec((tq, MXU_DIM), lambda b, h, i: (b * nq + i, h)),
                  pl.BlockSpec((MXU_DIM, seq), lambda b, h, i: (b * blocks_per_batch + h, 0)),
                  pl.BlockSpec((MXU_DIM, seq), lambda b, h, i: (b * blocks_per_batch + h, 0)),
                  pl.BlockSpec((tk, tk), lambda b, h, i: (0, 0))],
        out_specs=pl.BlockSpec((tq, MXU_DIM), lambda b, h, i: (b * nq + i, h)),
        scratch_shapes=[pltpu.VMEM((seq // tk, MXU_DIM, tk), BF16),
                        pltpu.VMEM((seq // tk, MXU_DIM, SB_HEADS_PER_BLOCK * tk), BF16),
                        pltpu.VMEM((tq, MXU_DIM), F32)],
        compiler_params=_params("arbitrary", "arbitrary", "arbitrary"),
        name="sb_prompt_attention",
    )(bias.astype(F32), q, kt, vt, tri)


def _head_rows(x_row, n_heads):
    width = x_row.shape[1]
    lane_head = lax.broadcasted_iota(I32, (n_heads, width), 1) // HEAD_DIM
    row = lax.broadcasted_iota(I32, (n_heads, width), 0)
    return jnp.where(lane_head == row, jnp.broadcast_to(x_row, (n_heads, width)), 0.0)


def _head_diag(acc):
    n_heads, width = acc.shape
    lane_head = lax.broadcasted_iota(I32, (n_heads, width), 1) // HEAD_DIM
    row = lax.broadcasted_iota(I32, (n_heads, width), 0)
    return jnp.sum(jnp.where(lane_head == row, acc, 0.0), axis=0, keepdims=True)


def _sb_sample_kernel(pt_ref, q_ref, bias_ref, tri_ref, *refs, n_pages_step, n_heads):
    del pt_ref
    k_refs = refs[:n_pages_step]
    v_refs = refs[n_pages_step:2 * n_pages_step]
    o_ref, acc_ref, carry_ref = refs[2 * n_pages_step:]
    s = pl.program_id(1)

    @pl.when(s == 0)
    def _():
        acc_ref[...] = jnp.zeros_like(acc_ref)
        carry_ref[...] = jnp.zeros_like(carry_ref)

    qh = _head_rows(q_ref[0], n_heads).astype(BF16)
    kcat = jnp.concatenate([r[...].astype(BF16) for r in k_refs], axis=1)
    vcat = jnp.concatenate([r[...].astype(BF16) for r in v_refs], axis=1)
    z = _dot(qh, kcat) + bias_ref[...]
    sp, ls = _softplus_parts(z)
    sp_hi = sp.astype(BF16)
    sp_lo = (sp - sp_hi.astype(F32)).astype(BF16)
    after = _dot(sp_hi, tri_ref[...]) + _dot(sp_lo, tri_ref[...]) + carry_ref[...]
    w = jnp.exp(ls - after)
    acc_ref[...] += _dot_nt(w.astype(BF16), vcat)
    carry_ref[...] += jnp.sum(sp, axis=1, keepdims=True)

    @pl.when(s == pl.num_programs(1) - 1)
    def _():
        o_ref[0] = _head_diag(acc_ref[...])


def _sb_sample_attention(q, cache_k, cache_v, page_table, bias):
    bd, width = q.shape
    n_heads = width // HEAD_DIM
    page = cache_k.shape[2]
    n_pages = page_table.shape[1]
    pps = SB_PAGES_PER_STEP
    n_steps = n_pages // pps
    idx = jnp.arange(pps * page)
    tri = (idx[:, None] > idx[None, :]).astype(BF16)

    def page_map(u):
        return lambda b, s, pt: (pt[b * n_pages + (n_steps - 1 - s) * pps + u], 0, 0)

    kv_specs = [pl.BlockSpec((None, width, page), page_map(u)) for u in range(pps)]
    out = pl.pallas_call(
        functools.partial(_sb_sample_kernel, n_pages_step=pps, n_heads=n_heads),
        out_shape=jax.ShapeDtypeStruct((bd, 1, width), F32),
        grid_spec=pltpu.PrefetchScalarGridSpec(
            num_scalar_prefetch=1,
            grid=(bd, n_steps),
            in_specs=[pl.BlockSpec((1, 1, width), lambda b, s, pt: (b, 0, 0)),
                      pl.BlockSpec((n_heads, 1), lambda b, s, pt: (0, 0)),
                      pl.BlockSpec((pps * page, pps * page), lambda b, s, pt: (0, 0))] + kv_specs + kv_specs,
            out_specs=pl.BlockSpec((1, 1, width), lambda b, s, pt: (b, 0, 0)),
            scratch_shapes=[pltpu.VMEM((n_heads, width), F32), pltpu.VMEM((n_heads, 1), F32)],
        ),
        compiler_params=_params("arbitrary", "arbitrary"),
        name="sb_sample_attention",
    )(page_table.reshape(-1).astype(I32), q.astype(F32).reshape(bd, 1, width),
      bias.astype(F32).reshape(n_heads, 1), tri, *([cache_k] * pps), *([cache_v] * pps))
    return out.reshape(bd, width)


def _dsw_prompt_kernel(slope_ref, *refs, seq):
    q_refs, k_refs, v_refs = refs[0:3], refs[3:6], refs[6:9]
    o_ref, og_ref, lg_ref = refs[9:]
    hp = pl.program_id(1)
    blk_rows = DSW_BLOCK
    n_groups = len(DSW_DILATIONS)
    heads_per_group = slope_ref.shape[0] // n_groups
    lane = lax.broadcasted_iota(I32, (1, LANES), 1)
    first_head = lane < HEAD_DIM

    qrow = lax.broadcasted_iota(I32, (2 * blk_rows, 2 * blk_rows), 0) % blk_rows
    kcol = lax.broadcasted_iota(I32, (2 * blk_rows, 2 * blk_rows), 1)
    steps = qrow + blk_rows - kcol
    in_band = (steps >= 0) & (steps <= DSW_SPAN)
    steps_f = steps.astype(F32)
    upper = lax.broadcasted_iota(I32, (2 * blk_rows, 1), 0) >= blk_rows

    for g, dil in enumerate(DSW_DILATIONS):
        n_blk = seq // dil // blk_rows
        s0 = slope_ref[g * heads_per_group + 2 * hp]
        s1 = slope_ref[g * heads_per_group + 2 * hp + 1]
        penalty = jnp.where(upper, s1, s0) * steps_f
        q_ref, k_ref, v_ref = q_refs[g], k_refs[g], v_refs[g]

        def rows_at(first, dil=dil):
            if dil == 1:
                return pl.ds(pl.multiple_of(first, blk_rows), blk_rows)
            return pl.ds(first, blk_rows, stride=dil)

        def block(t, _, g=g, dil=dil, n_blk=n_blk, penalty=penalty, q_ref=q_ref, k_ref=k_ref,
                  v_ref=v_ref, rows_at=rows_at):
            res = t // n_blk
            blk = t % n_blk
            own = res + dil * blk * blk_rows
            prev = res + dil * jnp.maximum(blk - 1, 0) * blk_rows
            qb = q_ref[rows_at(own), :]
            kc = jnp.concatenate([k_ref[rows_at(prev), :], k_ref[rows_at(own), :]], axis=0)
            vc = jnp.concatenate([v_ref[rows_at(prev), :], v_ref[rows_at(own), :]], axis=0)
            qm = jnp.concatenate([jnp.where(first_head, qb, 0.0), jnp.where(first_head, 0.0, qb)], axis=0)
            sc = _dot_nt(qm.astype(BF16), kc.astype(BF16)) - penalty
            valid = in_band & ((kcol >= blk_rows) | (blk > 0))
            sc = jnp.where(valid, sc, NEG_BIG)
            m = jnp.max(sc, axis=1, keepdims=True)
            p = jnp.exp(sc - m)
            l = jnp.sum(p, axis=1, keepdims=True)
            pb = p.astype(BF16)
            pcat = jnp.concatenate([pb[:blk_rows], pb[blk_rows:]], axis=1)
            vm = jnp.concatenate([jnp.where(first_head, vc, 0.0), jnp.where(first_head, 0.0, vc)], axis=0)
            o = _dot(pcat, vm.astype(BF16))
            lse = m + jnp.log(l)
            l_b = jnp.where(first_head, l[:blk_rows], l[blk_rows:])
            lse_b = jnp.where(first_head, lse[:blk_rows], lse[blk_rows:])
            og_ref[g, rows_at(own), :] = o / l_b
            lg_ref[g, rows_at(own), :] = lse_b
            return 0

        n_iter = dil * n_blk // DSW_BLOCKS_PER_ITER
        assert n_iter * DSW_BLOCKS_PER_ITER == dil * n_blk

        def blocks(it, carry, block=block):
            for u in range(DSW_BLOCKS_PER_ITER):
                block(it * DSW_BLOCKS_PER_ITER + u, carry)
            return carry

        lax.fori_loop(0, n_iter, blocks, 0)

    chunk = 2 * blk_rows
    for c in range(seq // chunk):
        rows = pl.ds(c * chunk, chunk)
        ls = [lg_ref[g, rows, :] for g in range(n_groups)]
        m = functools.reduce(jnp.maximum, ls)
        es = [jnp.exp(x - m) for x in ls]
        num = sum(e * og_ref[g, rows, :] for g, e in enumerate(es))
        o_ref[rows, :] = (num / sum(es)).astype(o_ref.dtype)


def _dsw_prompt_attention(q, k, v, slopes, *, batch, seq):
    n_groups = len(DSW_DILATIONS)
    width = q.shape[1] // n_groups
    blocks_per_group = width // LANES

    def col_map(g):
        return lambda b, h: (b, g * blocks_per_group + h)

    specs = [pl.BlockSpec((seq, LANES), col_map(g)) for g in range(n_groups)]
    return pl.pallas_call(
        functools.partial(_dsw_prompt_kernel, seq=seq),
        out_shape=jax.ShapeDtypeStruct((batch * seq, width), BF16),
        grid=(batch, blocks_per_group),
        in_specs=[pl.BlockSpec(memory_space=pltpu.SMEM)] + specs * 3,
        out_specs=pl.BlockSpec((seq, LANES), lambda b, h: (b, h)),
        scratch_shapes=[pltpu.VMEM((n_groups, seq, LANES), F32), pltpu.VMEM((n_groups, seq, LANES), F32)],
        compiler_params=_params("arbitrary", "arbitrary"),
        name="dsw_prompt_attention",
    )(slopes, *([q] * n_groups), *([k] * n_groups), *([v] * n_groups))


def _dsw_sample_kernel(slope_ref, q_ref, kn_ref, vn_ref, *refs, rows, heads_per_block):
    n_groups = len(rows)
    buf_refs = refs[:n_groups]
    o_ref = refs[n_groups]
    out_refs = refs[n_groups + 1:2 * n_groups + 1]
    p_refs = refs[2 * n_groups + 1:3 * n_groups + 1]
    pn_ref, alpha_ref = refs[3 * n_groups + 1:]
    j = pl.program_id(1)
    nq = pl.num_programs(1) // 2
    hb = heads_per_block
    width = hb * HEAD_DIM

    def column(row):
        eye = lax.broadcasted_iota(I32, (width, width), 0) == lax.broadcasted_iota(I32, (width, width), 1)
        return jnp.sum(jnp.where(eye, jnp.broadcast_to(row, (width, width)), 0.0), axis=1, keepdims=True)

    def moved_down(x, new_row):
        n = x.shape[1]
        lane = lax.broadcasted_iota(I32, (1, n), 1)
        return jnp.where(lane == n - 1, column(new_row), pltpu.roll(x, n - 1, 1))

    def rounded(x):
        return x.astype(BF16).astype(F32)

    @pl.when(j < nq)
    def _():
        lses, pns = [], []
        for g, dil in enumerate(DSW_DILATIONS):
            kt = buf_refs[g][...]
            qh = _head_rows(q_ref[0, pl.ds(g * nq + j, 1), :], hb)
            k_new = kn_ref[0, pl.ds(g * nq + j, 1), :]
            row = lax.broadcasted_iota(I32, (1, rows[g]), 1)
            back = (rows[g] - row).astype(F32)
            sc = _dot(qh.astype(BF16), kt.astype(BF16)) - slope_ref[g] * back
            sc = jnp.where((row & (dil - 1)) == 0, sc, NEG_BIG)
            sn = jnp.sum(rounded(qh) * rounded(k_new), axis=1, keepdims=True)
            m = jnp.maximum(jnp.max(sc, axis=1, keepdims=True), sn)
            lse = m + jnp.log(jnp.sum(jnp.exp(sc - m), axis=1, keepdims=True) + jnp.exp(sn - m))
            p_refs[g][j] = jnp.exp(sc - lse)
            pns.append(jnp.exp(sn - lse))
            lses.append(lse)
            out_refs[g][...] = moved_down(kt, k_new)
        m = functools.reduce(jnp.maximum, lses)
        es = [jnp.exp(l - m) for l in lses]
        pn_ref[j] = jnp.concatenate(pns, axis=1)
        alpha_ref[j] = jnp.concatenate(es, axis=1) / sum(es)

    @pl.when(j >= nq)
    def _():
        jq = j - nq
        pn = rounded(pn_ref[jq])
        alpha = rounded(alpha_ref[jq])
        acc = jnp.zeros((hb, width), F32)
        for g in range(n_groups):
            vt = buf_refs[g][...]
            v_new = vn_ref[0, pl.ds(g * nq + jq, 1), :]
            o_g = _dot_nt(p_refs[g][jq].astype(BF16), vt.astype(BF16)) + pn[:, g:g + 1] * rounded(v_new)
            acc = acc + alpha[:, g:g + 1] * rounded(o_g)
            out_refs[g][...] = moved_down(vt, v_new)
        o_ref[0] = _head_diag(acc)


def _dsw_sample_attention(q, k_new, v_new, bufs_t, slopes):
    bd, total = q.shape
    n_groups = len(DSW_DILATIONS)
    width = total // n_groups
    rows = tuple(b.shape[2] for b in bufs_t)
    for g, dil in enumerate(DSW_DILATIONS):
        assert rows[g] == DSW_SPAN * dil
    hb = DSW_SAMPLE_HEADS
    block = hb * HEAD_DIM
    nq = width // block
    blocked = lambda a: a.astype(F32).reshape(bd, n_groups * nq, block)
    vec_spec = pl.BlockSpec((1, n_groups * nq, block), lambda b, j: (b, 0, 0))
    buf_specs = [pl.BlockSpec((None, block, r), lambda b, j: (b, j, 0)) for r in rows]
    outs = pl.pallas_call(
        functools.partial(_dsw_sample_kernel, rows=rows, heads_per_block=hb),
        out_shape=(jax.ShapeDtypeStruct((bd * nq, 1, block), F32),)
        + tuple(jax.ShapeDtypeStruct(b.shape, F32) for b in bufs_t),
        grid=(bd, 2 * nq),
        in_specs=[pl.BlockSpec((n_groups, None, hb, 1), lambda b, j: (0, jnp.minimum(j, nq - 1), 0, 0)),
                  vec_spec, vec_spec, vec_spec] + buf_specs,
        out_specs=(pl.BlockSpec((1, 1, block), lambda b, j: (b * nq + jnp.maximum(j - nq, 0), 0, 0)),)
        + tuple(buf_specs),
        scratch_shapes=[pltpu.VMEM((nq, hb, r), F32) for r in rows]
        + [pltpu.VMEM((nq, hb, n_groups), F32), pltpu.VMEM((nq, hb, n_groups), F32)],
        compiler_params=_params("arbitrary", "arbitrary"),
        name="dsw_sample_attention",
    )(slopes.reshape(n_groups, nq, hb, 1), blocked(q), blocked(k_new), blocked(v_new), *bufs_t)
    return outs[0].reshape(bd, width), outs[1:]


def _router_kernel(hp_ref, hs_ref, g_ref, w_ref, b_ref, xn_ref, eid_ref, gate_ref, *, n_groups, n_experts):
    i = pl.program_id(0)
    n_prompt_tiles = pl.num_programs(0) - 1
    lane = lax.broadcasted_iota(I32, (1, LANES), 1)
    lane_f = lane.astype(F32)
    big = float(LANES)

    def first_lane(hit):
        return jnp.min(jnp.where(hit, lane_f, big), axis=1, keepdims=True).astype(I32)

    def route(x):
        xn = _rms(x, g_ref[...])
        logits = _dot(xn.astype(BF16), w_ref[...]) + b_ref[...]
        gl = jnp.where(lane < n_groups, logits, -jnp.inf)
        gmax = jnp.max(gl, axis=1, keepdims=True)
        gsel = first_lane(gl == gmax)
        gden = jnp.sum(jnp.exp(gl - gmax), axis=1, keepdims=True)
        first = n_groups + gsel * n_experts
        el = jnp.where((lane >= first) & (lane < first + n_experts), logits, -jnp.inf)
        m1 = jnp.max(el, axis=1, keepdims=True)
        i1 = first_lane(el == m1)
        el2 = jnp.where(lane == i1, -jnp.inf, el)
        m2 = jnp.max(el2, axis=1, keepdims=True)
        i2 = first_lane(el2 == m2)
        e2 = jnp.exp(m2 - m1)
        w1 = 1.0 / (1.0 + e2) / gden
        w2 = e2 / (1.0 + e2) / gden
        eid = jnp.where(lane == 0, i1 - n_groups, jnp.where(lane == 1, i2 - n_groups, 0))
        gate = jnp.where(lane == 0, w1, jnp.where(lane == 1, w2, 0.0))
        return xn, eid, gate

    def store_token_major(xn):
        for s in range(xn.shape[1] // LANES):
            xn_ref[pl.ds(s, xn.shape[0], stride=TOKEN_ROWS), :] = xn[:, s * LANES:(s + 1) * LANES]

    @pl.when(i < n_prompt_tiles)
    def _():
        xn, eid, gate = route(hp_ref[...])
        store_token_major(xn)
        eid_ref[...] = eid
        gate_ref[...] = gate

    @pl.when(i == n_prompt_tiles)
    def _():
        ns = hs_ref.shape[0]
        xn, eid, gate = route(hs_ref[...])
        xn_ref[...] = jnp.zeros_like(xn_ref)
        eid_ref[...] = jnp.zeros_like(eid_ref)
        gate_ref[...] = jnp.zeros_like(gate_ref)
        store_token_major(xn)
        eid_ref[0:ns, :] = eid
        gate_ref[0:ns, :] = gate


def _router(h_prompt, h_sample, g, w_router, b_router, *, n_groups, n_experts, tm):
    tp, d = h_prompt.shape
    assert d == TOKEN_ROWS * LANES
    ns = h_sample.shape[0]
    n_tiles = tp // tm + 1
    last = tp // tm - 1
    return pl.pallas_call(
        functools.partial(_router_kernel, n_groups=n_groups, n_experts=n_experts),
        out_shape=(jax.ShapeDtypeStruct((n_tiles * tm * TOKEN_ROWS, LANES), F32),
                   jax.ShapeDtypeStruct((n_tiles * tm, LANES), I32),
                   jax.ShapeDtypeStruct((n_tiles * tm, LANES), F32)),
        grid=(n_tiles,),
        in_specs=[pl.BlockSpec((tm, d), lambda i: (jnp.minimum(i, last), 0)),
                  pl.BlockSpec((ns, d), lambda i: (0, 0)),
                  pl.BlockSpec((1, d), lambda i: (0, 0)),
                  pl.BlockSpec((d, LANES), lambda i: (0, 0)),
                  pl.BlockSpec((1, LANES), lambda i: (0, 0))],
        out_specs=(pl.BlockSpec((tm * TOKEN_ROWS, LANES), lambda i: (i, 0)),
                   pl.BlockSpec((tm, LANES), lambda i: (i, 0)),
                   pl.BlockSpec((tm, LANES), lambda i: (i, 0))),
        compiler_params=_params("arbitrary"),
        name="moe_router",
    )(h_prompt, h_sample, g.reshape(1, d), w_router, b_router)


def _moe_kernel(tile_e_ref, nvalid_ref, rows_ref, slot_ref, slot_next_ref, gate_ref, x_hbm,
                wg_ref, wu_ref, wd_ref, y_hbm, xbuf, ybuf, wgb, wub, wdb, gsem, ssem):
    i = pl.program_id(0)
    nv = nvalid_ref[0]
    tm = xbuf.shape[1] // TOKEN_ROWS
    n_chunks = TOKEN_ROWS
    slot = lax.rem(i, 2)

    def tile_rows(first):
        return pl.ds(pl.multiple_of(first * TOKEN_ROWS, TOKEN_ROWS), TOKEN_ROWS)

    def gather(idx_ref, s):
        def issue(r, _):
            pltpu.make_async_copy(x_hbm.at[tile_rows(idx_ref[0, 0, r] >> 1), :], xbuf.at[s, tile_rows(r), :],
                                  gsem.at[s]).start()
            return 0
        lax.fori_loop(0, tm, issue, 0, unroll=8)

    def gather_done(s):
        return pltpu.make_async_copy(x_hbm.at[pl.ds(0, tm * TOKEN_ROWS), :], xbuf.at[s], gsem.at[s])

    def scatter_wait(s, tile):
        n = pl.multiple_of(rows_ref[tile] * TOKEN_ROWS, TOKEN_ROWS)
        pltpu.make_async_copy(ybuf.at[s, pl.ds(0, n), :], y_hbm.at[pl.ds(0, n), :], ssem.at[s]).wait()

    @pl.when(i == 0)
    def _():
        gather(slot_ref, 0)

    @pl.when(i < nv)
    def _():
        gather_done(slot).wait()

        @pl.when(i + 1 < nv)
        def _():
            gather(slot_next_ref, 1 - slot)

        @pl.when((i == 0) | (tile_e_ref[i] != tile_e_ref[jnp.maximum(i - 1, 0)]))
        def _():
            wgb[...] = wg_ref[...].astype(BF16)
            wub[...] = wu_ref[...].astype(BF16)
            wdb[...] = wd_ref[...].astype(BF16)

        x = jnp.concatenate([xbuf[slot, pl.ds(c, tm, stride=TOKEN_ROWS), :] for c in range(n_chunks)],
                            axis=1).astype(BF16)
        hg = _dot(x, wgb[...])
        hu = _dot(x, wub[...])
        h = (hg / (1.0 + jnp.exp(-hg))) * hu * gate_ref[...]
        y = _dot(h.astype(BF16), wdb[...])

        @pl.when(i >= 2)
        def _():
            scatter_wait(slot, i - 2)

        for c in range(n_chunks):
            ybuf[slot, pl.ds(c, tm, stride=TOKEN_ROWS), :] = y[:, c * LANES:(c + 1) * LANES]
        n_rows = rows_ref[i]

        def issue(r, _):
            @pl.when(r < n_rows)
            def _():
                pltpu.make_async_copy(ybuf.at[slot, tile_rows(r), :], y_hbm.at[tile_rows(slot_ref[0, 0, r]), :],
                                      ssem.at[slot]).start()
            return 0
        lax.fori_loop(0, tm, issue, 0, unroll=8)

        @pl.when(i == nv - 1)
        def _():
            scatter_wait(slot, i)

            @pl.when(i >= 1)
            def _():
                scatter_wait(1 - slot, i - 1)


def _moe_experts(xn_all, eid, gate, w_gate, w_up, w_down, *, n_real, first_expert, n_exp):
    assert MOE_TOP_K == 2
    _, d, d_ff = w_gate.shape
    tm = MOE_TILE
    n_slots = n_real * MOE_TOP_K
    n_tiles = -(-n_slots // tm) + n_exp
    n_pos = n_tiles * tm

    e_flat = eid.reshape(-1)
    tok_flat = jnp.repeat(jnp.arange(n_real, dtype=I32), MOE_TOP_K)
    k_flat = jnp.tile(jnp.arange(MOE_TOP_K, dtype=I32), n_real)
    onehot = (e_flat[:, None] == jnp.arange(n_exp, dtype=I32)[None, :]).astype(I32)
    csum = jnp.cumsum(onehot, axis=0)
    rank = jnp.sum((csum - 1) * onehot, axis=1)
    counts = csum[-1]
    tiles_e = (counts + tm - 1) // tm
    tiles_end = jnp.cumsum(tiles_e)
    tile_start = tiles_end - tiles_e
    pos = jnp.sum(onehot * tile_start[None, :], axis=1) * tm + rank
    n_valid = tiles_end[-1]
    tile_ids = jnp.arange(n_tiles, dtype=I32)
    tile_e = jnp.minimum(jnp.sum((tiles_end[None, :] <= tile_ids[:, None]).astype(I32), axis=1), n_exp - 1)
    tile_first = jnp.sum((tile_e[:, None] == jnp.arange(n_exp, dtype=I32)[None, :]) * tile_start[None, :], axis=1)
    tile_count = jnp.sum((tile_e[:, None] == jnp.arange(n_exp, dtype=I32)[None, :]) * counts[None, :], axis=1)
    tile_rows = jnp.where(tile_ids < n_valid, jnp.clip(tile_count - (tile_ids - tile_first) * tm, 0, tm), 0).astype(I32)
    tile_e = jnp.where(tile_ids < n_valid, tile_e, tile_e[jnp.maximum(n_valid - 1, 0)]).astype(I32) + first_expert
    slot_flat = tok_flat * MOE_TOP_K + k_flat
    packed = jnp.stack([slot_flat, lax.bitcast_convert_type(gate.reshape(-1).astype(F32), I32)], axis=1)
    table = jnp.zeros((n_pos, 2), I32).at[pos].set(packed, unique_indices=True)
    slot3 = table[:, 0].reshape(n_tiles, 1, tm)
    gate_tbl = lax.bitcast_convert_type(table[:, 1], F32).reshape(n_pos, 1)

    smem_tile = lambda off: pl.BlockSpec(
        (1, 1, tm), lambda i, te, nv, tr: (jnp.minimum(i + off, n_tiles - 1), 0, 0), memory_space=pltpu.SMEM)
    expert_block = lambda shape: pl.BlockSpec((None,) + shape, lambda i, te, nv, tr: (te[i], 0, 0))
    return pl.pallas_call(
        _moe_kernel,
        out_shape=jax.ShapeDtypeStruct((n_slots * TOKEN_ROWS, LANES), F32),
        grid_spec=pltpu.PrefetchScalarGridSpec(
            num_scalar_prefetch=3,
            grid=(n_tiles,),
            in_specs=[smem_tile(0), smem_tile(1),
                      pl.BlockSpec((tm, 1), lambda i, te, nv, tr: (i, 0)),
                      pl.BlockSpec(memory_space=pl.ANY),
                      expert_block((d, d_ff)), expert_block((d, d_ff)), expert_block((d_ff, d))],
            out_specs=pl.BlockSpec(memory_space=pl.ANY),
            scratch_shapes=[pltpu.VMEM((2, tm * TOKEN_ROWS, LANES), F32), pltpu.VMEM((2, tm * TOKEN_ROWS, LANES), F32),
                            pltpu.VMEM((d, d_ff), BF16), pltpu.VMEM((d, d_ff), BF16), pltpu.VMEM((d_ff, d), BF16),
                            pltpu.SemaphoreType.DMA((2,)), pltpu.SemaphoreType.DMA((2,))],
        ),
        compiler_params=_params("arbitrary"),
        name="moe_experts",
    )(tile_e, n_valid.reshape(1).astype(I32), tile_rows, slot3, slot3, gate_tbl, xn_all, w_gate, w_up, w_down)


def _combine_kernel(h_ref, y_ref, o_ref):
    tm = h_ref.shape[0]
    stride = MOE_TOP_K * TOKEN_ROWS
    for c in range(TOKEN_ROWS):
        cols = slice(c * LANES, (c + 1) * LANES)
        o_ref[:, cols] = h_ref[:, cols] + (y_ref[pl.ds(c, tm, stride=stride), :]
                                           + y_ref[pl.ds(TOKEN_ROWS + c, tm, stride=stride), :])


def _combine(h, y_tiles, *, first_row, tm):
    t, d = h.shape
    b0 = first_row // tm
    rows = tm * MOE_TOP_K * TOKEN_ROWS
    return pl.pallas_call(
        _combine_kernel,
        out_shape=jax.ShapeDtypeStruct((t, d), F32),
        grid=(t // tm,),
        in_specs=[pl.BlockSpec((tm, d), lambda i: (i, 0)),
                  pl.BlockSpec((rows, LANES), lambda i: (b0 + i, 0))],
        out_specs=pl.BlockSpec((tm, d), lambda i: (i, 0)),
        compiler_params=_params("arbitrary"),
        name="moe_combine",
    )(h, y_tiles)


def _hier_moe(hp, hs, g, w_group, b_group, w_expert, b_expert, w_gate, w_up, w_down, layer):
    tp, d = hp.shape
    ns = hs.shape[0]
    n_groups, _, n_experts = w_expert.shape
    n_exp = n_groups * n_experts
    n_route = n_groups * (1 + n_experts)
    w_router = jnp.concatenate(
        [w_group, jnp.moveaxis(w_expert, 0, 1).reshape(d, n_groups * n_experts),
         jnp.zeros((d, LANES - n_route), F32)], axis=1).astype(BF16)
    b_router = jnp.concatenate(
        [b_group, b_expert.reshape(-1), jnp.zeros((LANES - n_route,), F32)]).astype(F32).reshape(1, LANES)
    xn_all, eid, gate = _router(hp, hs, g, w_router, b_router, n_groups=n_groups, n_experts=n_experts, tm=ROW_TILE)
    n_real = tp + ns
    d_ff = w_gate.shape[-1]
    y_tiles = _moe_experts(xn_all, eid[:n_real, :MOE_TOP_K], gate[:n_real, :MOE_TOP_K],
                           w_gate.reshape(-1, d, d_ff), w_up.reshape(-1, d, d_ff), w_down.reshape(-1, d_ff, d),
                           n_real=n_real, first_expert=layer * n_exp, n_exp=n_exp)
    return (_combine(hp, y_tiles, first_row=0, tm=ROW_TILE),
            _combine(hs, y_tiles, first_row=tp, tm=ns))


def kernel(x_prompt, x_sample, cache_sb_k, cache_sb_v, cache_dsw0_kv, cache_dsw1_kv, cache_dsw2_kv, page_table, norm1_g, norm2_g, sb_w_in, sb_logit_bias, sb_w_out, dsw_w_in, dsw_q_norm_g, dsw_k_norm_g, dsw_w_out, moe_w_group, moe_b_group, moe_w_expert, moe_b_expert, moe_w_gate, moe_w_up, moe_w_down):
    batch, seq, d = x_prompt.shape
    bd, dec_seq, _ = x_sample.shape
    assert dec_seq == 1
    depth = norm1_g.shape[0]
    dsw_caches = (cache_dsw0_kv, cache_dsw1_kv, cache_dsw2_kv)
    n_groups = len(DSW_DILATIONS)
    sb_heads = cache_sb_k.shape[3]
    dsw_heads = cache_dsw0_kv.shape[4]
    page = cache_sb_k.shape[2]
    tp = batch * seq

    hp = x_prompt.reshape(tp, d)
    hs = x_sample.reshape(bd, d)
    sb_kp, sb_vp, sb_ks, sb_vs = [], [], [], []
    dsw_p = [[] for _ in range(n_groups)]
    dsw_s = [[] for _ in range(n_groups)]

    for layer in range(depth):
        if layer % 2 == 0:
            la = layer // 2
            w_in = sb_w_in[la].astype(BF16)
            w_out = sb_w_out[la].astype(BF16)
            width = sb_heads * HEAD_DIM
            qp, kpt, vpt = _qkv_proj_t(hp, norm1_g[layer], w_in.T, batch=batch, seq=seq, tm=ROW_TILE)
            qs, ks, vs = _qkv_proj(hs, norm1_g[layer], w_in, tm=bd, q_dtype=F32)
            op = _sb_prompt_attention(qp, kpt, vpt, sb_logit_bias[la], batch=batch, seq=seq)
            pages_t = lambda c: jnp.transpose(c[:, la], (0, 2, 3, 1)).reshape(-1, width, page)
            os_ = _sb_sample_attention(qs, pages_t(cache_sb_k), pages_t(cache_sb_v), page_table, sb_logit_bias[la])
            hp = _out_proj(op, w_out, hp, tm=ROW_TILE)
            hs = _out_proj(os_, w_out, hs, tm=bd)
            seq_major = lambda a: jnp.transpose(a.reshape(batch, sb_heads, HEAD_DIM, seq), (0, 3, 1, 2))
            sb_kp.append(seq_major(kpt))
            sb_vp.append(seq_major(vpt))
            sb_ks.append(ks.reshape(bd, 1, sb_heads, HEAD_DIM))
            sb_vs.append(vs.reshape(bd, 1, sb_heads, HEAD_DIM))
        else:
            lb = layer // 2
            w_in = dsw_w_in[lb].astype(BF16)
            w_out = dsw_w_out[lb].astype(BF16)
            gains = (dsw_q_norm_g[lb], dsw_k_norm_g[lb])
            width = dsw_heads * HEAD_DIM
            n = jnp.arange(1, n_groups * dsw_heads + 1, dtype=F32)
            slopes = (2.0 ** (-8.0 * n / (n_groups * dsw_heads))).reshape(n_groups, dsw_heads)
            step_penalty = slopes * jnp.asarray(DSW_DILATIONS, F32)[:, None]
            qp, kp, vp = _qkv_proj(hp, norm1_g[layer], w_in, tm=ROW_TILE, q_dtype=F32, head_gains=gains)
            qs, ks, vs = _qkv_proj(hs, norm1_g[layer], w_in, tm=bd, q_dtype=F32, head_gains=gains)
            op = _dsw_prompt_attention(qp, kp, vp, step_penalty.reshape(-1), batch=batch, seq=seq)
            bufs_t = [jnp.transpose(c[lb], (0, 2, 3, 4, 1)).reshape(bd, 2 * width, c.shape[2]) for c in dsw_caches]
            os_, new_bufs_t = _dsw_sample_attention(qs, ks, vs, bufs_t, slopes)
            hp = _out_proj(op, w_out, hp, tm=ROW_TILE)
            hs = _out_proj(os_, w_out, hs, tm=bd)
            kp4 = kp.reshape(batch, seq, n_groups, dsw_heads, HEAD_DIM)
            vp4 = vp.reshape(batch, seq, n_groups, dsw_heads, HEAD_DIM)
            for g in range(n_groups):
                keep = min(dsw_caches[g].shape[2], seq)
                dsw_p[g].append(jnp.stack([kp4[:, seq - keep:, g], vp4[:, seq - keep:, g]], axis=2))
                rows_g = new_bufs_t[g].shape[2]
                dsw_s[g].append(jnp.transpose(new_bufs_t[g].reshape(bd, 2, dsw_heads, HEAD_DIM, rows_g),
                                              (0, 4, 1, 2, 3)))
        hp, hs = _hier_moe(hp, hs, norm2_g[layer], moe_w_group[layer], moe_b_group[layer], moe_w_expert[layer],
                           moe_b_expert[layer], moe_w_gate, moe_w_up, moe_w_down, layer)

    return (hp.reshape(batch, seq, d), hs.reshape(bd, 1, d),
            jnp.stack(sb_kp, axis=1), jnp.stack(sb_vp, axis=1),
            jnp.stack(sb_ks, axis=1), jnp.stack(sb_vs, axis=1),
            jnp.stack(dsw_p[0], axis=0), jnp.stack(dsw_s[0], axis=0),
            jnp.stack(dsw_p[1], axis=0), jnp.stack(dsw_s[1], axis=0),
            jnp.stack(dsw_p[2], axis=0), jnp.stack(dsw_s[2], axis=0))
```

```python
import functools

import jax
import jax.numpy as jnp
from jax import lax
from jax.experimental import pallas as pl
from jax.experimental.pallas import tpu as pltpu

F32 = jnp.float32
BF16 = jnp.bfloat16
I32 = jnp.int32

HEAD_DIM = 64
DSW_DILATIONS = (1, 4, 16)
DSW_SPAN = 128
MOE_TOP_K = 2
NORM_EPS = 1e-6
Q_SCALE = HEAD_DIM ** -0.5

LANES = 128
TOKEN_ROWS = 8
MXU_DIM = 256
VMEM_LIMIT = 48 * 1024 * 1024

NEG_BIG = -1e30

ROW_TILE = 512
COL_TILE = 1024
DSW_QKV_ROW_TILE = 1024
MOE_TILE = 256
SB_Q_TILE = 256
SB_K_TILE = 256
SB_HEADS_PER_BLOCK = MXU_DIM // HEAD_DIM
SB_PAGES_PER_STEP = 8
SB_SAMPLE_SUB_BLOCK = 512
DSW_BLOCK = 128
DSW_BLOCKS_PER_ITER = 8
DSW_SAMPLE_HEADS = MXU_DIM // HEAD_DIM


def _params(*semantics):
    return pltpu.CompilerParams(dimension_semantics=semantics, vmem_limit_bytes=VMEM_LIMIT)


def _rms(x, g):
    ms = jnp.mean(x * x, axis=-1, keepdims=True)
    return (x * lax.rsqrt(ms + NORM_EPS)) * g


def _dot(a, b):
    return jnp.dot(a, b, preferred_element_type=F32)


def _dot_nt(a, b):
    return lax.dot_general(a, b, (((1,), (1,)), ((), ())), preferred_element_type=F32)


def _qkv_kernel(*refs, nb, head_norm, q_dtype):
    if head_norm:
        x_ref, g_ref, w_ref, gq_ref, gk_ref, bd_ref, q_ref, k_ref, v_ref, xn_ref = refs
    else:
        x_ref, g_ref, w_ref, q_ref, k_ref, v_ref, xn_ref = refs
    j = pl.program_id(1)

    @pl.when(j == 0)
    def _():
        xn_ref[...] = _rms(x_ref[...], g_ref[...]).astype(BF16)

    y = _dot(xn_ref[...], w_ref[...])

    def head_normed(y, gain_ref):
        parts = []
        for c in range(y.shape[1] // MXU_DIM):
            yc = y[:, c * MXU_DIM:(c + 1) * MXU_DIM]
            sq = yc * yc
            hi = sq.astype(BF16)
            lo = (sq - hi.astype(F32)).astype(BF16)
            ms = _dot(hi, bd_ref[...]) + _dot(lo, bd_ref[...])
            parts.append(yc * lax.rsqrt(ms + NORM_EPS))
        return jnp.concatenate(parts, axis=1) * gain_ref[...]

    @pl.when(j < nb)
    def _():
        q = head_normed(y, gq_ref) if head_norm else y
        q_ref[...] = (q * Q_SCALE).astype(q_dtype)

    @pl.when((j >= nb) & (j < 2 * nb))
    def _():
        k_ref[...] = head_normed(y, gk_ref) if head_norm else y

    @pl.when(j >= 2 * nb)
    def _():
        v_ref[...] = y


def _qkv_proj(x, g, w_bf16, *, tm, q_dtype, head_gains=None):
    t, d = x.shape
    n = w_bf16.shape[1]
    p = n // 3
    nb = p // COL_TILE
    head_norm = head_gains is not None
    in_specs = [
        pl.BlockSpec((tm, d), lambda i, j: (i, 0)),
        pl.BlockSpec((1, d), lambda i, j: (0, 0)),
        pl.BlockSpec((d, COL_TILE), lambda i, j: (0, j)),
    ]
    args = [x, g.reshape(1, d), w_bf16]
    if head_norm:
        gq, gk = head_gains
        reps = COL_TILE // HEAD_DIM
        head_of = jnp.arange(MXU_DIM) // HEAD_DIM
        bd = jnp.where(head_of[:, None] == head_of[None, :], 1.0 / HEAD_DIM, 0.0).astype(BF16)
        in_specs += [
            pl.BlockSpec((1, COL_TILE), lambda i, j: (0, 0)),
            pl.BlockSpec((1, COL_TILE), lambda i, j: (0, 0)),
            pl.BlockSpec((MXU_DIM, MXU_DIM), lambda i, j: (0, 0)),
        ]
        args += [jnp.tile(gq.astype(F32), reps).reshape(1, COL_TILE),
                 jnp.tile(gk.astype(F32), reps).reshape(1, COL_TILE), bd]

    def part_map(first):
        return lambda i, j: (i, jnp.clip(j - first, 0, nb - 1))

    return pl.pallas_call(
        functools.partial(_qkv_kernel, nb=nb, head_norm=head_norm, q_dtype=q_dtype),
        out_shape=(jax.ShapeDtypeStruct((t, p), q_dtype),
                   jax.ShapeDtypeStruct((t, p), F32),
                   jax.ShapeDtypeStruct((t, p), F32)),
        grid=(t // tm, 3 * nb),
        in_specs=in_specs,
        out_specs=(pl.BlockSpec((tm, COL_TILE), part_map(0)),
                   pl.BlockSpec((tm, COL_TILE), part_map(nb)),
                   pl.BlockSpec((tm, COL_TILE), part_map(2 * nb))),
        scratch_shapes=[pltpu.VMEM((tm, d), BF16)],
        compiler_params=_params("arbitrary", "arbitrary"),
        name="qkv_proj",
    )(*args)


def _qkv_t_kernel(x_ref, g_ref, wt_ref, q_ref, kt_ref, vt_ref, xn_ref):
    j = pl.program_id(1)

    @pl.when(j == 0)
    def _():
        xn_ref[...] = _rms(x_ref[...], g_ref[...]).astype(BF16)
        q_ref[...] = (_dot_nt(xn_ref[...], wt_ref[...]) * Q_SCALE).astype(q_ref.dtype)

    @pl.when(j == 1)
    def _():
        kt_ref[...] = _dot_nt(wt_ref[...], xn_ref[...])

    @pl.when(j == 2)
    def _():
        vt_ref[...] = _dot_nt(wt_ref[...], xn_ref[...])


def _qkv_proj_t(x, g, wt_bf16, *, batch, seq, tm):
    t, d = x.shape
    p = wt_bf16.shape[0] // 3
    tiles_per_seq = seq // tm
    kv_map = lambda i, j: (i // tiles_per_seq, i % tiles_per_seq)
    return pl.pallas_call(
        _qkv_t_kernel,
        out_shape=(jax.ShapeDtypeStruct((t, p), BF16),
                   jax.ShapeDtypeStruct((batch * p, seq), F32),
                   jax.ShapeDtypeStruct((batch * p, seq), F32)),
        grid=(t // tm, 3),
        in_specs=[pl.BlockSpec((tm, d), lambda i, j: (i, 0)),
                  pl.BlockSpec((1, d), lambda i, j: (0, 0)),
                  pl.BlockSpec((p, d), lambda i, j: (j, 0))],
        out_specs=(pl.BlockSpec((tm, p), lambda i, j: (i, 0)),
                   pl.BlockSpec((p, tm), kv_map),
                   pl.BlockSpec((p, tm), kv_map)),
        scratch_shapes=[pltpu.VMEM((tm, d), BF16)],
        compiler_params=_params("arbitrary", "arbitrary"),
        name="qkv_proj_t",
    )(x, g.reshape(1, d), wt_bf16)


def _kv_state_kernel(x_ref, g_ref, wk_ref, wv_ref, gk_ref, bd_ref, o_ref):
    p = wk_ref.shape[0]
    xn = _rms(x_ref[...], g_ref[...]).astype(BF16)
    kt = _dot_nt(wk_ref[...], xn)
    for c in range(p // MXU_DIM):
        rows = slice(c * MXU_DIM, (c + 1) * MXU_DIM)
        kc = kt[rows]
        sq = kc * kc
        hi = sq.astype(BF16)
        lo = (sq - hi.astype(F32)).astype(BF16)
        ms = _dot(bd_ref[...], hi) + _dot(bd_ref[...], lo)
        o_ref[rows, :] = (kc * lax.rsqrt(ms + NORM_EPS)) * gk_ref[rows, :]
    o_ref[p:, :] = _dot_nt(wv_ref[...], xn)


def _kv_state_t(x, g, wt_bf16, gk, *, group, n_groups, batch, seq, keep):
    t, d = x.shape
    p = wt_bf16.shape[0] // (3 * n_groups)
    tm = min(keep, ROW_TILE)
    tiles = keep // tm
    first = (seq - keep) // tm
    per_seq = seq // tm
    head_of = jnp.arange(MXU_DIM) // HEAD_DIM
    bd = jnp.where(head_of[:, None] == head_of[None, :], 1.0 / HEAD_DIM, 0.0).astype(BF16)
    gk_col = jnp.tile(gk.astype(F32), p // HEAD_DIM).reshape(p, 1)
    return pl.pallas_call(
        _kv_state_kernel,
        out_shape=jax.ShapeDtypeStruct((batch * 2 * p, keep), F32),
        grid=(batch, tiles),
        in_specs=[pl.BlockSpec((tm, d), lambda b, i: (b * per_seq + first + i, 0)),
                  pl.BlockSpec((1, d), lambda b, i: (0, 0)),
                  pl.BlockSpec((p, d), lambda b, i: (n_groups + group, 0)),
                  pl.BlockSpec((p, d), lambda b, i: (2 * n_groups + group, 0)),
                  pl.BlockSpec((p, 1), lambda b, i: (0, 0)),
                  pl.BlockSpec((MXU_DIM, MXU_DIM), lambda b, i: (0, 0))],
        out_specs=pl.BlockSpec((2 * p, tm), lambda b, i: (b, i)),
        compiler_params=_params("arbitrary", "arbitrary"),
        name="dsw_kv_state",
    )(x, g.reshape(1, d), wt_bf16, wt_bf16, gk_col, bd)


def _out_proj_kernel(a_ref, w_ref, res_ref, o_ref):
    o_ref[...] = res_ref[...] + _dot(a_ref[...].astype(BF16), w_ref[...])


def _out_proj(a, w_bf16, res, *, tm):
    t, d_in = a.shape
    d = w_bf16.shape[1]
    return pl.pallas_call(
        _out_proj_kernel,
        out_shape=jax.ShapeDtypeStruct((t, d), F32),
        grid=(t // tm,),
        in_specs=[pl.BlockSpec((tm, d_in), lambda i: (i, 0)),
                  pl.BlockSpec((d_in, d), lambda i: (0, 0)),
                  pl.BlockSpec((tm, d), lambda i: (i, 0))],
        out_specs=pl.BlockSpec((tm, d), lambda i: (i, 0)),
        compiler_params=_params("arbitrary"),
        name="out_proj",
    )(a, w_bf16, res)


def _softplus_parts(z):
    sp = jnp.maximum(z, 0.0) + jnp.log(1.0 + jnp.exp(-jnp.abs(z)))
    return sp, z - sp


def _sb_prompt_kernel(bias_ref, q_ref, kt_ref, vt_ref, tri_ref, o_ref, kb_ref, vm_ref, acc_ref, *, tq, tk):
    hq = pl.program_id(1)
    qi = pl.program_id(2)
    nh = SB_HEADS_PER_BLOCK
    s_len = kt_ref.shape[1]
    head_of_lane = lax.broadcasted_iota(I32, (1, MXU_DIM), 1) // HEAD_DIM
    head_of_row = lax.broadcasted_iota(I32, (MXU_DIM, 1), 0) // HEAD_DIM

    @pl.when(qi == 0)
    def _():
        for j in range(s_len // tk):
            cols = slice(j * tk, (j + 1) * tk)
            kb_ref[j] = kt_ref[:, cols].astype(BF16)
            vj = vt_ref[:, cols]
            vm_ref[j] = jnp.concatenate(
                [jnp.where(head_of_row == h, vj, 0.0) for h in range(nh)], axis=1).astype(BF16)

    q = q_ref[...]
    qm = jnp.concatenate([jnp.where(head_of_lane == h, q, jnp.zeros_like(q)) for h in range(nh)], axis=0)
    bias = jnp.concatenate([jnp.full((tq, 1), bias_ref[hq * nh + h], F32) for h in range(nh)], axis=0)
    tri = tri_ref[...]
    acc_ref[...] = jnp.zeros_like(acc_ref)

    def step(j, carry, mask):
        z = _dot(qm, kb_ref[j]) + bias
        sp, ls = _softplus_parts(z)
        if mask is not None:
            sp = jnp.where(mask, sp, 0.0)
        after = _dot(sp.astype(BF16), tri) + carry
        w = jnp.exp(ls - after)
        if mask is not None:
            w = jnp.where(mask, w, 0.0)
        wb = w.astype(BF16)
        wcat = jnp.concatenate([wb[h * tq:(h + 1) * tq] for h in range(nh)], axis=1)
        acc_ref[...] += _dot_nt(wcat, vm_ref[j])
        return carry + jnp.sum(sp, axis=1, keepdims=True)

    row = lax.broadcasted_iota(I32, (nh * tq, tk), 0) % tq
    col = lax.broadcasted_iota(I32, (nh * tq, tk), 1)
    diag = col < row
    carry = step(qi, jnp.zeros((nh * tq, 1), F32), diag)
    lax.fori_loop(0, qi, lambda t, c: step(qi - 1 - t, c, None), carry)
    o_ref[...] = acc_ref[...].astype(o_ref.dtype)


def _sb_prompt_attention(q, kt, vt, bias, *, batch, seq):
    width = q.shape[1]
    blocks_per_batch = width // MXU_DIM
    tq, tk = SB_Q_TILE, SB_K_TILE
    nq = seq // tq
    idx = jnp.arange(tk)
    tri = (idx[:, None] > idx[None, :]).astype(BF16)
    return pl.pallas_call(
        functools.partial(_sb_prompt_kernel, tq=tq, tk=tk),
        out_shape=jax.ShapeDtypeStruct((batch * seq, width), BF16),
        grid=(batch, width // MXU_DIM, nq),
        in_specs=[pl.BlockSpec(memory_space=pltpu.SMEM),
                  pl.BlockSpec((tq, MXU_DIM), lambda b, h, i: (b * nq + i, h)),
                  pl.BlockSpec((MXU_DIM, seq), lambda b, h, i: (b * blocks_per_batch + h, 0)),
                  pl.BlockSpec((MXU_DIM, seq), lambda b, h, i: (b * blocks_per_batch + h, 0)),
                  pl.BlockSpec((tk, tk), lambda b, h, i: (0, 0))],
        out_specs=pl.BlockSpec((tq, MXU_DIM), lambda b, h, i: (b * nq + i, h)),
        scratch_shapes=[pltpu.VMEM((seq // tk, MXU_DIM, tk), BF16),
                        pltpu.VMEM((seq // tk, MXU_DIM, SB_HEADS_PER_BLOCK * tk), BF16),
                        pltpu.VMEM((tq, MXU_DIM), F32)],
        compiler_params=_params("arbitrary", "arbitrary", "arbitrary"),
        name="sb_prompt_attention",
    )(bias.astype(F32), q, kt, vt, tri)


def _head_rows(x_row, n_heads):
    width = x_row.shape[1]
    lane_head = lax.broadcasted_iota(I32, (n_heads, width), 1) // HEAD_DIM
    row = lax.broadcasted_iota(I32, (n_heads, width), 0)
    return jnp.where(lane_head == row, jnp.broadcast_to(x_row, (n_heads, width)), 0.0)


def _head_diag(acc):
    n_heads, width = acc.shape
    lane_head = lax.broadcasted_iota(I32, (n_heads, width), 1) // HEAD_DIM
    row = lax.broadcasted_iota(I32, (n_heads, width), 0)
    return jnp.sum(jnp.where(lane_head == row, acc, 0.0), axis=0, keepdims=True)


def _sb_sample_kernel(pt_ref, q_ref, bias_ref, tri_ref, *refs, n_pages_step, n_heads):
    del pt_ref
    k_refs = refs[:n_pages_step]
    v_refs = refs[n_pages_step:2 * n_pages_step]
    o_ref, acc_ref, carry_ref = refs[2 * n_pages_step:]
    s = pl.program_id(1)

    @pl.when(s == 0)
    def _():
        acc_ref[...] = jnp.zeros_like(acc_ref)
        carry_ref[...] = jnp.zeros_like(carry_ref)

    qh = _head_rows(q_ref[0], n_heads).astype(BF16)
    kcat = jnp.concatenate([r[...].astype(BF16) for r in k_refs], axis=1)
    vcat = jnp.concatenate([r[...].astype(BF16) for r in v_refs], axis=1)
    z = _dot(qh, kcat) + bias_ref[...]
    sp, ls = _softplus_parts(z)
    sub = tri_ref.shape[0]
    carry = carry_ref[...]
    afters = []
    for c in reversed(range(z.shape[1] // sub)):
        sp_c = sp[:, c * sub:(c + 1) * sub]
        sp_hi = sp_c.astype(BF16)
        sp_lo = (sp_c - sp_hi.astype(F32)).astype(BF16)
        afters.append(_dot(sp_hi, tri_ref[...]) + _dot(sp_lo, tri_ref[...]) + carry)
        carry = carry + jnp.sum(sp_c, axis=1, keepdims=True)
    after = jnp.concatenate(afters[::-1], axis=1)
    w = jnp.exp(ls - after)
    acc_ref[...] += _dot_nt(w.astype(BF16), vcat)
    carry_ref[...] = carry

    @pl.when(s == pl.num_programs(1) - 1)
    def _():
        o_ref[0] = _head_diag(acc_ref[...])


def _sb_sample_attention(q, cache_k, cache_v, page_table, bias):
    bd, width = q.shape
    n_heads = width // HEAD_DIM
    page = cache_k.shape[2]
    n_pages = page_table.shape[1]
    pps = SB_PAGES_PER_STEP
    n_steps = n_pages // pps
    idx = jnp.arange(SB_SAMPLE_SUB_BLOCK)
    tri = (idx[:, None] > idx[None, :]).astype(BF16)
    assert (pps * page) % SB_SAMPLE_SUB_BLOCK == 0

    def page_map(u):
        return lambda b, s, pt: (pt[b * n_pages + (n_steps - 1 - s) * pps + u], 0, 0)

    kv_specs = [pl.BlockSpec((None, width, page), page_map(u)) for u in range(pps)]
    out = pl.pallas_call(
        functools.partial(_sb_sample_kernel, n_pages_step=pps, n_heads=n_heads),
        out_shape=jax.ShapeDtypeStruct((bd, 1, width), F32),
        grid_spec=pltpu.PrefetchScalarGridSpec(
            num_scalar_prefetch=1,
            grid=(bd, n_steps),
            in_specs=[pl.BlockSpec((1, 1, width), lambda b, s, pt: (b, 0, 0)),
                      pl.BlockSpec((n_heads, 1), lambda b, s, pt: (0, 0)),
                      pl.BlockSpec((SB_SAMPLE_SUB_BLOCK, SB_SAMPLE_SUB_BLOCK), lambda b, s, pt: (0, 0))]
            + kv_specs + kv_specs,
            out_specs=pl.BlockSpec((1, 1, width), lambda b, s, pt: (b, 0, 0)),
            scratch_shapes=[pltpu.VMEM((n_heads, width), F32), pltpu.VMEM((n_heads, 1), F32)],
        ),
        compiler_params=_params("arbitrary", "arbitrary"),
        name="sb_sample_attention",
    )(page_table.reshape(-1).astype(I32), q.astype(F32).reshape(bd, 1, width),
      bias.astype(F32).reshape(n_heads, 1), tri, *([cache_k] * pps), *([cache_v] * pps))
    return out.reshape(bd, width)


def _dsw_prompt_kernel(slope_ref, *refs, seq):
    q_refs, k_refs, v_refs = refs[0:3], refs[3:6], refs[6:9]
    o_ref, og_ref, lg_ref = refs[9:]
    hp = pl.program_id(1)
    blk_rows = DSW_BLOCK
    n_groups = len(DSW_DILATIONS)
    heads_per_group = slope_ref.shape[0] // n_groups
    lane = lax.broadcasted_iota(I32, (1, LANES), 1)
    first_head = lane < HEAD_DIM

    qrow = lax.broadcasted_iota(I32, (2 * blk_rows, 2 * blk_rows), 0) % blk_rows
    kcol = lax.broadcasted_iota(I32, (2 * blk_rows, 2 * blk_rows), 1)
    steps = qrow + blk_rows - kcol
    in_band = (steps >= 0) & (steps <= DSW_SPAN)
    steps_f = steps.astype(F32)
    upper = lax.broadcasted_iota(I32, (2 * blk_rows, 1), 0) >= blk_rows

    for g, dil in enumerate(DSW_DILATIONS):
        n_blk = seq // dil // blk_rows
        s0 = slope_ref[g * heads_per_group + 2 * hp]
        s1 = slope_ref[g * heads_per_group + 2 * hp + 1]
        penalty = jnp.where(upper, s1, s0) * steps_f
        q_ref, k_ref, v_ref = q_refs[g], k_refs[g], v_refs[g]

        def rows_at(first, dil=dil):
            if dil == 1:
                return pl.ds(pl.multiple_of(first, blk_rows), blk_rows)
            return pl.ds(first, blk_rows, stride=dil)

        def block(t, _, g=g, dil=dil, n_blk=n_blk, penalty=penalty, q_ref=q_ref, k_ref=k_ref,
                  v_ref=v_ref, rows_at=rows_at):
            res = t // n_blk
            blk = t % n_blk
            own = res + dil * blk * blk_rows
            prev = res + dil * jnp.maximum(blk - 1, 0) * blk_rows
            qb = q_ref[rows_at(own), :]
            kc = jnp.concatenate([k_ref[rows_at(prev), :], k_ref[rows_at(own), :]], axis=0)
            vc = jnp.concatenate([v_ref[rows_at(prev), :], v_ref[rows_at(own), :]], axis=0)
            qm = jnp.concatenate([jnp.where(first_head, qb, 0.0), jnp.where(first_head, 0.0, qb)], axis=0)
            sc = _dot_nt(qm.astype(BF16), kc.astype(BF16)) - penalty
            valid = in_band & ((kcol >= blk_rows) | (blk > 0))
            sc = jnp.where(valid, sc, NEG_BIG)
            m = jnp.max(sc, axis=1, keepdims=True)
            p = jnp.exp(sc - m)
            l = jnp.sum(p, axis=1, keepdims=True)
            pb = p.astype(BF16)
            pcat = jnp.concatenate([pb[:blk_rows], pb[blk_rows:]], axis=1)
            vm = jnp.concatenate([jnp.where(first_head, vc, 0.0), jnp.where(first_head, 0.0, vc)], axis=0)
            o = _dot(pcat, vm.astype(BF16))
            lse = m + jnp.log(l)
            l_b = jnp.where(first_head, l[:blk_rows], l[blk_rows:])
            lse_b = jnp.where(first_head, lse[:blk_rows], lse[blk_rows:])
            og_ref[g, rows_at(own), :] = o / l_b
            lg_ref[g, rows_at(own), :] = lse_b
            return 0

        n_iter = dil * n_blk // DSW_BLOCKS_PER_ITER
        assert n_iter * DSW_BLOCKS_PER_ITER == dil * n_blk

        def blocks(it, carry, block=block):
            for u in range(DSW_BLOCKS_PER_ITER):
                block(it * DSW_BLOCKS_PER_ITER + u, carry)
            return carry

        lax.fori_loop(0, n_iter, blocks, 0)

    chunk = 2 * blk_rows
    for c in range(seq // chunk):
        rows = pl.ds(c * chunk, chunk)
        ls = [lg_ref[g, rows, :] for g in range(n_groups)]
        m = functools.reduce(jnp.maximum, ls)
        es = [jnp.exp(x - m) for x in ls]
        num = sum(e * og_ref[g, rows, :] for g, e in enumerate(es))
        o_ref[rows, :] = (num / sum(es)).astype(o_ref.dtype)


def _dsw_prompt_attention(q, k, v, slopes, *, batch, seq):
    n_groups = len(DSW_DILATIONS)
    width = q.shape[1] // n_groups
    blocks_per_group = width // LANES

    def col_map(g):
        return lambda b, h: (b, g * blocks_per_group + h)

    specs = [pl.BlockSpec((seq, LANES), col_map(g)) for g in range(n_groups)]
    return pl.pallas_call(
        functools.partial(_dsw_prompt_kernel, seq=seq),
        out_shape=jax.ShapeDtypeStruct((batch * seq, width), BF16),
        grid=(batch, blocks_per_group),
        in_specs=[pl.BlockSpec(memory_space=pltpu.SMEM)] + specs * 3,
        out_specs=pl.BlockSpec((seq, LANES), lambda b, h: (b, h)),
        scratch_shapes=[pltpu.VMEM((n_groups, seq, LANES), F32), pltpu.VMEM((n_groups, seq, LANES), F32)],
        compiler_params=_params("arbitrary", "arbitrary"),
        name="dsw_prompt_attention",
    )(slopes, *([q] * n_groups), *([k] * n_groups), *([v] * n_groups))


def _dsw_sample_kernel(slope_ref, q_ref, kn_ref, vn_ref, *refs, rows, heads_per_block):
    n_groups = len(rows)
    buf_refs = refs[:n_groups]
    o_ref = refs[n_groups]
    out_refs = refs[n_groups + 1:2 * n_groups + 1]
    p_refs = refs[2 * n_groups + 1:3 * n_groups + 1]
    pn_ref, alpha_ref = refs[3 * n_groups + 1:]
    j = pl.program_id(1)
    nq = pl.num_programs(1) // 2
    hb = heads_per_block
    width = hb * HEAD_DIM

    def column(row):
        eye = lax.broadcasted_iota(I32, (width, width), 0) == lax.broadcasted_iota(I32, (width, width), 1)
        return jnp.sum(jnp.where(eye, jnp.broadcast_to(row, (width, width)), 0.0), axis=1, keepdims=True)

    def moved_down(x, new_row):
        n = x.shape[1]
        lane = lax.broadcasted_iota(I32, (1, n), 1)
        return jnp.where(lane == n - 1, column(new_row), pltpu.roll(x, n - 1, 1))

    def rounded(x):
        return x.astype(BF16).astype(F32)

    @pl.when(j < nq)
    def _():
        lses, pns = [], []
        for g, dil in enumerate(DSW_DILATIONS):
            kt = buf_refs[g][...]
            qh = _head_rows(q_ref[0, pl.ds(g * nq + j, 1), :], hb)
            k_new = kn_ref[0, pl.ds(g * nq + j, 1), :]
            row = lax.broadcasted_iota(I32, (1, rows[g]), 1)
            back = (rows[g] - row).astype(F32)
            sc = _dot(qh.astype(BF16), kt.astype(BF16)) - slope_ref[g] * back
            sc = jnp.where((row & (dil - 1)) == 0, sc, NEG_BIG)
            sn = jnp.sum(rounded(qh) * rounded(k_new), axis=1, keepdims=True)
            m = jnp.maximum(jnp.max(sc, axis=1, keepdims=True), sn)
            lse = m + jnp.log(jnp.sum(jnp.exp(sc - m), axis=1, keepdims=True) + jnp.exp(sn - m))
            p_refs[g][j] = jnp.exp(sc - lse)
            pns.append(jnp.exp(sn - lse))
            lses.append(lse)
            out_refs[g][...] = moved_down(kt, k_new)
        m = functools.reduce(jnp.maximum, lses)
        es = [jnp.exp(l - m) for l in lses]
        pn_ref[j] = jnp.concatenate(pns, axis=1)
        alpha_ref[j] = jnp.concatenate(es, axis=1) / sum(es)

    @pl.when(j >= nq)
    def _():
        jq = j - nq
        pn = rounded(pn_ref[jq])
        alpha = rounded(alpha_ref[jq])
        acc = jnp.zeros((hb, width), F32)
        for g in range(n_groups):
            vt = buf_refs[g][...]
            v_new = vn_ref[0, pl.ds(g * nq + jq, 1), :]
            o_g = _dot_nt(p_refs[g][jq].astype(BF16), vt.astype(BF16)) + pn[:, g:g + 1] * rounded(v_new)
            acc = acc + alpha[:, g:g + 1] * rounded(o_g)
            out_refs[g][...] = moved_down(vt, v_new)
        o_ref[0] = _head_diag(acc)


def _dsw_sample_attention(q, k_new, v_new, bufs_t, slopes):
    bd, total = q.shape
    n_groups = len(DSW_DILATIONS)
    width = total // n_groups
    rows = tuple(b.shape[2] for b in bufs_t)
    for g, dil in enumerate(DSW_DILATIONS):
        assert rows[g] == DSW_SPAN * dil
    hb = DSW_SAMPLE_HEADS
    block = hb * HEAD_DIM
    nq = width // block
    blocked = lambda a: a.astype(F32).reshape(bd, n_groups * nq, block)
    vec_spec = pl.BlockSpec((1, n_groups * nq, block), lambda b, j: (b, 0, 0))
    buf_specs = [pl.BlockSpec((None, block, r), lambda b, j: (b, j, 0)) for r in rows]
    outs = pl.pallas_call(
        functools.partial(_dsw_sample_kernel, rows=rows, heads_per_block=hb),
        out_shape=(jax.ShapeDtypeStruct((bd * nq, 1, block), F32),)
        + tuple(jax.ShapeDtypeStruct(b.shape, F32) for b in bufs_t),
        grid=(bd, 2 * nq),
        in_specs=[pl.BlockSpec((n_groups, None, hb, 1), lambda b, j: (0, jnp.minimum(j, nq - 1), 0, 0)),
                  vec_spec, vec_spec, vec_spec] + buf_specs,
        out_specs=(pl.BlockSpec((1, 1, block), lambda b, j: (b * nq + jnp.maximum(j - nq, 0), 0, 0)),)
        + tuple(buf_specs),
        scratch_shapes=[pltpu.VMEM((nq, hb, r), F32) for r in rows]
        + [pltpu.VMEM((nq, hb, n_groups), F32), pltpu.VMEM((nq, hb, n_groups), F32)],
        compiler_params=_params("arbitrary", "arbitrary"),
        name="dsw_sample_attention",
    )(slopes.reshape(n_groups, nq, hb, 1), blocked(q), blocked(k_new), blocked(v_new), *bufs_t)
    return outs[0].reshape(bd, width), outs[1:]


def _router_kernel(hp_ref, hs_ref, g_ref, w_ref, b_ref, xn_ref, eid_ref, gate_ref, *, n_groups, n_experts):
    i = pl.program_id(0)
    n_prompt_tiles = pl.num_programs(0) - 1
    lane = lax.broadcasted_iota(I32, (1, LANES), 1)
    lane_f = lane.astype(F32)
    big = float(LANES)

    def first_lane(hit):
        return jnp.min(jnp.where(hit, lane_f, big), axis=1, keepdims=True).astype(I32)

    def route(x):
        xn = _rms(x, g_ref[...])
        logits = _dot(xn.astype(BF16), w_ref[...]) + b_ref[...]
        gl = jnp.where(lane < n_groups, logits, -jnp.inf)
        gmax = jnp.max(gl, axis=1, keepdims=True)
        gsel = first_lane(gl == gmax)
        gden = jnp.sum(jnp.exp(gl - gmax), axis=1, keepdims=True)
        first = n_groups + gsel * n_experts
        el = jnp.where((lane >= first) & (lane < first + n_experts), logits, -jnp.inf)
        m1 = jnp.max(el, axis=1, keepdims=True)
        i1 = first_lane(el == m1)
        el2 = jnp.where(lane == i1, -jnp.inf, el)
        m2 = jnp.max(el2, axis=1, keepdims=True)
        i2 = first_lane(el2 == m2)
        e2 = jnp.exp(m2 - m1)
        w1 = 1.0 / (1.0 + e2) / gden
        w2 = e2 / (1.0 + e2) / gden
        eid = jnp.where(lane == 0, i1 - n_groups, jnp.where(lane == 1, i2 - n_groups, 0))
        gate = jnp.where(lane == 0, w1, jnp.where(lane == 1, w2, 0.0))
        return xn, eid, gate

    def store_token_major(xn):
        for s in range(xn.shape[1] // LANES):
            xn_ref[pl.ds(s, xn.shape[0], stride=TOKEN_ROWS), :] = xn[:, s * LANES:(s + 1) * LANES]

    @pl.when(i < n_prompt_tiles)
    def _():
        xn, eid, gate = route(hp_ref[...])
        store_token_major(xn)
        eid_ref[...] = eid
        gate_ref[...] = gate

    @pl.when(i == n_prompt_tiles)
    def _():
        ns = hs_ref.shape[0]
        xn, eid, gate = route(hs_ref[...])
        xn_ref[...] = jnp.zeros_like(xn_ref)
        eid_ref[...] = jnp.zeros_like(eid_ref)
        gate_ref[...] = jnp.zeros_like(gate_ref)
        store_token_major(xn)
        eid_ref[0:ns, :] = eid
        gate_ref[0:ns, :] = gate


def _router(h_prompt, h_sample, g, w_router, b_router, *, n_groups, n_experts, tm):
    tp, d = h_prompt.shape
    assert d == TOKEN_ROWS * LANES
    ns = h_sample.shape[0]
    n_tiles = tp // tm + 1
    last = tp // tm - 1
    return pl.pallas_call(
        functools.partial(_router_kernel, n_groups=n_groups, n_experts=n_experts),
        out_shape=(jax.ShapeDtypeStruct((n_tiles * tm * TOKEN_ROWS, LANES), F32),
                   jax.ShapeDtypeStruct((n_tiles * tm, LANES), I32),
                   jax.ShapeDtypeStruct((n_tiles * tm, LANES), F32)),
        grid=(n_tiles,),
        in_specs=[pl.BlockSpec((tm, d), lambda i: (jnp.minimum(i, last), 0)),
                  pl.BlockSpec((ns, d), lambda i: (0, 0)),
                  pl.BlockSpec((1, d), lambda i: (0, 0)),
                  pl.BlockSpec((d, LANES), lambda i: (0, 0)),
                  pl.BlockSpec((1, LANES), lambda i: (0, 0))],
        out_specs=(pl.BlockSpec((tm * TOKEN_ROWS, LANES), lambda i: (i, 0)),
                   pl.BlockSpec((tm, LANES), lambda i: (i, 0)),
                   pl.BlockSpec((tm, LANES), lambda i: (i, 0))),
        compiler_params=_params("arbitrary"),
        name="moe_router",
    )(h_prompt, h_sample, g.reshape(1, d), w_router, b_router)


def _moe_kernel(tile_e_ref, nvalid_ref, rows_ref, slot_ref, slot_next_ref, gate_ref, x_hbm,
                wg_ref, wu_ref, wd_ref, y_hbm, xbuf, ybuf, wgb, wub, wdb, gsem, ssem):
    i = pl.program_id(0)
    nv = nvalid_ref[0]
    tm = xbuf.shape[1] // TOKEN_ROWS
    n_chunks = TOKEN_ROWS
    slot = lax.rem(i, 2)

    def tile_rows(first):
        return pl.ds(pl.multiple_of(first * TOKEN_ROWS, TOKEN_ROWS), TOKEN_ROWS)

    def gather(idx_ref, s):
        def issue(r, _):
            pltpu.make_async_copy(x_hbm.at[tile_rows(idx_ref[0, 0, r] >> 1), :], xbuf.at[s, tile_rows(r), :],
                                  gsem.at[s]).start()
            return 0
        lax.fori_loop(0, tm, issue, 0, unroll=8)

    def gather_done(s):
        return pltpu.make_async_copy(x_hbm.at[pl.ds(0, tm * TOKEN_ROWS), :], xbuf.at[s], gsem.at[s])

    def scatter_wait(s, tile):
        n = pl.multiple_of(rows_ref[tile] * TOKEN_ROWS, TOKEN_ROWS)
        pltpu.make_async_copy(ybuf.at[s, pl.ds(0, n), :], y_hbm.at[pl.ds(0, n), :], ssem.at[s]).wait()

    @pl.when(i == 0)
    def _():
        gather(slot_ref, 0)

    @pl.when(i < nv)
    def _():
        gather_done(slot).wait()

        @pl.when(i + 1 < nv)
        def _():
            gather(slot_next_ref, 1 - slot)

        @pl.when((i == 0) | (tile_e_ref[i] != tile_e_ref[jnp.maximum(i - 1, 0)]))
        def _():
            wgb[...] = wg_ref[...].astype(BF16)
            wub[...] = wu_ref[...].astype(BF16)
            wdb[...] = wd_ref[...].astype(BF16)

        x = jnp.concatenate([xbuf[slot, pl.ds(c, tm, stride=TOKEN_ROWS), :] for c in range(n_chunks)],
                            axis=1).astype(BF16)
        hg = _dot(x, wgb[...])
        hu = _dot(x, wub[...])
        h = (hg / (1.0 + jnp.exp(-hg))) * hu * gate_ref[...]
        y = _dot(h.astype(BF16), wdb[...])

        @pl.when(i >= 2)
        def _():
            scatter_wait(slot, i - 2)

        for c in range(n_chunks):
            ybuf[slot, pl.ds(c, tm, stride=TOKEN_ROWS), :] = y[:, c * LANES:(c + 1) * LANES]
        n_rows = rows_ref[i]

        def issue(r, _):
            @pl.when(r < n_rows)
            def _():
                pltpu.make_async_copy(ybuf.at[slot, tile_rows(r), :], y_hbm.at[tile_rows(slot_ref[0, 0, r]), :],
                                      ssem.at[slot]).start()
            return 0
        lax.fori_loop(0, tm, issue, 0, unroll=8)

        @pl.when(i == nv - 1)
        def _():
            scatter_wait(slot, i)

            @pl.when(i >= 1)
            def _():
                scatter_wait(1 - slot, i - 1)


def _moe_experts(xn_all, eid, gate, w_gate, w_up, w_down, *, n_real, first_expert, n_exp):
    assert MOE_TOP_K == 2
    _, d, d_ff = w_gate.shape
    tm = MOE_TILE
    n_slots = n_real * MOE_TOP_K
    n_tiles = -(-n_slots // tm) + n_exp
    n_pos = n_tiles * tm

    e_flat = eid.reshape(-1)
    tok_flat = jnp.repeat(jnp.arange(n_real, dtype=I32), MOE_TOP_K)
    k_flat = jnp.tile(jnp.arange(MOE_TOP_K, dtype=I32), n_real)
    onehot = (e_flat[:, None] == jnp.arange(n_exp, dtype=I32)[None, :]).astype(I32)
    csum = jnp.cumsum(onehot, axis=0)
    rank = jnp.sum((csum - 1) * onehot, axis=1)
    counts = csum[-1]
    tiles_e = (counts + tm - 1) // tm
    tiles_end = jnp.cumsum(tiles_e)
    tile_start = tiles_end - tiles_e
    pos = jnp.sum(onehot * tile_start[None, :], axis=1) * tm + rank
    n_valid = tiles_end[-1]
    tile_ids = jnp.arange(n_tiles, dtype=I32)
    tile_e = jnp.minimum(jnp.sum((tiles_end[None, :] <= tile_ids[:, None]).astype(I32), axis=1), n_exp - 1)
    tile_first = jnp.sum((tile_e[:, None] == jnp.arange(n_exp, dtype=I32)[None, :]) * tile_start[None, :], axis=1)
    tile_count = jnp.sum((tile_e[:, None] == jnp.arange(n_exp, dtype=I32)[None, :]) * counts[None, :], axis=1)
    tile_rows = jnp.where(tile_ids < n_valid, jnp.clip(tile_count - (tile_ids - tile_first) * tm, 0, tm), 0).astype(I32)
    tile_e = jnp.where(tile_ids < n_valid, tile_e, tile_e[jnp.maximum(n_valid - 1, 0)]).astype(I32) + first_expert
    slot_flat = tok_flat * MOE_TOP_K + k_flat
    packed = jnp.stack([slot_flat, lax.bitcast_convert_type(gate.reshape(-1).astype(F32), I32)], axis=1)
    table = jnp.zeros((n_pos, 2), I32).at[pos].set(packed, unique_indices=True)
    slot3 = table[:, 0].reshape(n_tiles, 1, tm)
    gate_tbl = lax.bitcast_convert_type(table[:, 1], F32).reshape(n_pos, 1)

    smem_tile = lambda off: pl.BlockSpec(
        (1, 1, tm), lambda i, te, nv, tr: (jnp.minimum(i + off, n_tiles - 1), 0, 0), memory_space=pltpu.SMEM)
    expert_block = lambda shape: pl.BlockSpec((None,) + shape, lambda i, te, nv, tr: (te[i], 0, 0))
    return pl.pallas_call(
        _moe_kernel,
        out_shape=jax.ShapeDtypeStruct((n_slots * TOKEN_ROWS, LANES), F32),
        grid_spec=pltpu.PrefetchScalarGridSpec(
            num_scalar_prefetch=3,
            grid=(n_tiles,),
            in_specs=[smem_tile(0), smem_tile(1),
                      pl.BlockSpec((tm, 1), lambda i, te, nv, tr: (i, 0)),
                      pl.BlockSpec(memory_space=pl.ANY),
                      expert_block((d, d_ff)), expert_block((d, d_ff)), expert_block((d_ff, d))],
            out_specs=pl.BlockSpec(memory_space=pl.ANY),
            scratch_shapes=[pltpu.VMEM((2, tm * TOKEN_ROWS, LANES), F32), pltpu.VMEM((2, tm * TOKEN_ROWS, LANES), F32),
                            pltpu.VMEM((d, d_ff), BF16), pltpu.VMEM((d, d_ff), BF16), pltpu.VMEM((d_ff, d), BF16),
                            pltpu.SemaphoreType.DMA((2,)), pltpu.SemaphoreType.DMA((2,))],
        ),
        compiler_params=_params("arbitrary"),
        name="moe_experts",
    )(tile_e, n_valid.reshape(1).astype(I32), tile_rows, slot3, slot3, gate_tbl, xn_all, w_gate, w_up, w_down)


def _combine_kernel(h_ref, y_ref, o_ref):
    tm = h_ref.shape[0]
    stride = MOE_TOP_K * TOKEN_ROWS
    for c in range(TOKEN_ROWS):
        cols = slice(c * LANES, (c + 1) * LANES)
        o_ref[:, cols] = h_ref[:, cols] + (y_ref[pl.ds(c, tm, stride=stride), :]
                                           + y_ref[pl.ds(TOKEN_ROWS + c, tm, stride=stride), :])


def _combine(h, y_tiles, *, first_row, tm):
    t, d = h.shape
    b0 = first_row // tm
    rows = tm * MOE_TOP_K * TOKEN_ROWS
    return pl.pallas_call(
        _combine_kernel,
        out_shape=jax.ShapeDtypeStruct((t, d), F32),
        grid=(t // tm,),
        in_specs=[pl.BlockSpec((tm, d), lambda i: (i, 0)),
                  pl.BlockSpec((rows, LANES), lambda i: (b0 + i, 0))],
        out_specs=pl.BlockSpec((tm, d), lambda i: (i, 0)),
        compiler_params=_params("arbitrary"),
        name="moe_combine",
    )(h, y_tiles)


def _hier_moe(hp, hs, g, w_group, b_group, w_expert, b_expert, w_gate, w_up, w_down, layer):
    tp, d = hp.shape
    ns = hs.shape[0]
    n_groups, _, n_experts = w_expert.shape
    n_exp = n_groups * n_experts
    n_route = n_groups * (1 + n_experts)
    w_router = jnp.concatenate(
        [w_group, jnp.moveaxis(w_expert, 0, 1).reshape(d, n_groups * n_experts),
         jnp.zeros((d, LANES - n_route), F32)], axis=1).astype(BF16)
    b_router = jnp.concatenate(
        [b_group, b_expert.reshape(-1), jnp.zeros((LANES - n_route,), F32)]).astype(F32).reshape(1, LANES)
    xn_all, eid, gate = _router(hp, hs, g, w_router, b_router, n_groups=n_groups, n_experts=n_experts, tm=ROW_TILE)
    n_real = tp + ns
    d_ff = w_gate.shape[-1]
    y_tiles = _moe_experts(xn_all, eid[:n_real, :MOE_TOP_K], gate[:n_real, :MOE_TOP_K],
                           w_gate.reshape(-1, d, d_ff), w_up.reshape(-1, d, d_ff), w_down.reshape(-1, d_ff, d),
                           n_real=n_real, first_expert=layer * n_exp, n_exp=n_exp)
    return (_combine(hp, y_tiles, first_row=0, tm=ROW_TILE),
            _combine(hs, y_tiles, first_row=tp, tm=ns))


def kernel(x_prompt, x_sample, cache_sb_k, cache_sb_v, cache_dsw0_kv, cache_dsw1_kv, cache_dsw2_kv, page_table, norm1_g, norm2_g, sb_w_in, sb_logit_bias, sb_w_out, dsw_w_in, dsw_q_norm_g, dsw_k_norm_g, dsw_w_out, moe_w_group, moe_b_group, moe_w_expert, moe_b_expert, moe_w_gate, moe_w_up, moe_w_down):
    batch, seq, d = x_prompt.shape
    bd, dec_seq, _ = x_sample.shape
    assert dec_seq == 1
    depth = norm1_g.shape[0]
    dsw_caches = (cache_dsw0_kv, cache_dsw1_kv, cache_dsw2_kv)
    n_groups = len(DSW_DILATIONS)
    sb_heads = cache_sb_k.shape[3]
    dsw_heads = cache_dsw0_kv.shape[4]
    page = cache_sb_k.shape[2]
    tp = batch * seq

    hp = x_prompt.reshape(tp, d)
    hs = x_sample.reshape(bd, d)
    sb_kp, sb_vp, sb_ks, sb_vs = [], [], [], []
    dsw_p = [[] for _ in range(n_groups)]
    dsw_s = [[] for _ in range(n_groups)]

    for layer in range(depth):
        if layer % 2 == 0:
            la = layer // 2
            w_in = sb_w_in[la].astype(BF16)
            w_out = sb_w_out[la].astype(BF16)
            width = sb_heads * HEAD_DIM
            qp, kpt, vpt = _qkv_proj_t(hp, norm1_g[layer], w_in.T, batch=batch, seq=seq, tm=ROW_TILE)
            qs, ks, vs = _qkv_proj(hs, norm1_g[layer], w_in, tm=bd, q_dtype=F32)
            op = _sb_prompt_attention(qp, kpt, vpt, sb_logit_bias[la], batch=batch, seq=seq)
            pages_t = lambda c: jnp.transpose(c[:, la], (0, 2, 3, 1)).reshape(-1, width, page)
            os_ = _sb_sample_attention(qs, pages_t(cache_sb_k), pages_t(cache_sb_v), page_table, sb_logit_bias[la])
            hp = _out_proj(op, w_out, hp, tm=ROW_TILE)
            hs = _out_proj(os_, w_out, hs, tm=bd)
            seq_major = lambda a: jnp.transpose(a.reshape(batch, sb_heads, HEAD_DIM, seq), (0, 3, 1, 2))
            sb_kp.append(seq_major(kpt))
            sb_vp.append(seq_major(vpt))
            sb_ks.append(ks.reshape(bd, 1, sb_heads, HEAD_DIM))
            sb_vs.append(vs.reshape(bd, 1, sb_heads, HEAD_DIM))
        else:
            lb = layer // 2
            w_in = dsw_w_in[lb].astype(BF16)
            w_out = dsw_w_out[lb].astype(BF16)
            gains = (dsw_q_norm_g[lb], dsw_k_norm_g[lb])
            width = dsw_heads * HEAD_DIM
            n = jnp.arange(1, n_groups * dsw_heads + 1, dtype=F32)
            slopes = (2.0 ** (-8.0 * n / (n_groups * dsw_heads))).reshape(n_groups, dsw_heads)
            step_penalty = slopes * jnp.asarray(DSW_DILATIONS, F32)[:, None]
            qp, kp, vp = _qkv_proj(hp, norm1_g[layer], w_in, tm=DSW_QKV_ROW_TILE, q_dtype=F32, head_gains=gains)
            qs, ks, vs = _qkv_proj(hs, norm1_g[layer], w_in, tm=bd, q_dtype=F32, head_gains=gains)
            op = _dsw_prompt_attention(qp, kp, vp, step_penalty.reshape(-1), batch=batch, seq=seq)
            bufs_t = [jnp.transpose(c[lb], (0, 2, 3, 4, 1)).reshape(bd, 2 * width, c.shape[2]) for c in dsw_caches]
            os_, new_bufs_t = _dsw_sample_attention(qs, ks, vs, bufs_t, slopes)
            w_in_t = w_in.T
            for g in range(n_groups):
                keep = min(dsw_caches[g].shape[2], seq)
                state_t = _kv_state_t(hp, norm1_g[layer], w_in_t, gains[1], group=g, n_groups=n_groups,
                                      batch=batch, seq=seq, keep=keep)
                dsw_p[g].append(jnp.transpose(state_t.reshape(batch, 2, dsw_heads, HEAD_DIM, keep), (0, 4, 1, 2, 3)))
            hp = _out_proj(op, w_out, hp, tm=ROW_TILE)
            hs = _out_proj(os_, w_out, hs, tm=bd)
            for g in range(n_groups):
                rows_g = new_bufs_t[g].shape[2]
                dsw_s[g].append(jnp.transpose(new_bufs_t[g].reshape(bd, 2, dsw_heads, HEAD_DIM, rows_g),
                                              (0, 4, 1, 2, 3)))
        hp, hs = _hier_moe(hp, hs, norm2_g[layer], moe_w_group[layer], moe_b_group[layer], moe_w_expert[layer],
                           moe_b_expert[layer], moe_w_gate, moe_w_up, moe_w_down, layer)

    return (hp.reshape(batch, seq, d), hs.reshape(bd, 1, d),
            jnp.stack(sb_kp, axis=1), jnp.stack(sb_vp, axis=1),
            jnp.stack(sb_ks, axis=1), jnp.stack(sb_vs, axis=1),
            jnp.stack(dsw_p[0], axis=0), jnp.stack(dsw_s[0], axis=0),
            jnp.stack(dsw_p[1], axis=0), jnp.stack(dsw_s[1], axis=0),
            jnp.stack(dsw_p[2], axis=0), jnp.stack(dsw_s[2], axis=0))
```

```python
import functools

import jax
import jax.numpy as jnp
from jax import lax
from jax.experimental import pallas as pl
from jax.experimental.pallas import tpu as pltpu

F32 = jnp.float32
BF16 = jnp.bfloat16
I32 = jnp.int32

HEAD_DIM = 64
DSW_DILATIONS = (1, 4, 16)
DSW_SPAN = 128
MOE_TOP_K = 2
NORM_EPS = 1e-6
Q_SCALE = HEAD_DIM ** -0.5

LANES = 128
TOKEN_ROWS = 8
MXU_DIM = 256
VMEM_LIMIT = 48 * 1024 * 1024

NEG_BIG = -1e30

ROW_TILE = 512
COL_TILE = 1024
DSW_QKV_ROW_TILE = 1024
SB_QKV_ROW_TILE = 1024
MOE_TILE = 256
SB_Q_TILE = 256
SB_K_TILE = 256
SB_HEADS_PER_BLOCK = MXU_DIM // HEAD_DIM
SB_PAGES_PER_STEP = 8
SB_SAMPLE_SUB_BLOCK = 512
DSW_BLOCK = 128
DSW_BLOCKS_PER_ITER = 8
DSW_SAMPLE_HEADS = MXU_DIM // HEAD_DIM


def _params(*semantics):
    return pltpu.CompilerParams(dimension_semantics=semantics, vmem_limit_bytes=VMEM_LIMIT)


def _rms(x, g):
    ms = jnp.mean(x * x, axis=-1, keepdims=True)
    return (x * lax.rsqrt(ms + NORM_EPS)) * g


def _dot(a, b):
    return jnp.dot(a, b, preferred_element_type=F32)


def _dot_nt(a, b):
    return lax.dot_general(a, b, (((1,), (1,)), ((), ())), preferred_element_type=F32)


def _qkv_kernel(*refs, nb, head_norm):
    if head_norm:
        x_ref, g_ref, w_ref, gq_ref, gk_ref, bd_ref, q_ref, k_ref, v_ref, xn_ref = refs
    else:
        x_ref, g_ref, w_ref, q_ref, k_ref, v_ref, xn_ref = refs
    j = pl.program_id(1)

    @pl.when(j == 0)
    def _():
        xn_ref[...] = _rms(x_ref[...], g_ref[...]).astype(BF16)

    y = _dot(xn_ref[...], w_ref[...])

    def head_normed(y, gain_ref):
        parts = []
        for c in range(y.shape[1] // MXU_DIM):
            yc = y[:, c * MXU_DIM:(c + 1) * MXU_DIM]
            sq = yc * yc
            hi = sq.astype(BF16)
            lo = (sq - hi.astype(F32)).astype(BF16)
            ms = _dot(hi, bd_ref[...]) + _dot(lo, bd_ref[...])
            parts.append(yc * lax.rsqrt(ms + NORM_EPS))
        return jnp.concatenate(parts, axis=1) * gain_ref[...]

    @pl.when(j < nb)
    def _():
        q = head_normed(y, gq_ref) if head_norm else y
        q_ref[...] = q * Q_SCALE

    @pl.when((j >= nb) & (j < 2 * nb))
    def _():
        k_ref[...] = head_normed(y, gk_ref) if head_norm else y

    @pl.when(j >= 2 * nb)
    def _():
        v_ref[...] = y


def _qkv_proj(x, g, w_bf16, *, tm, head_gains=None):
    t, d = x.shape
    n = w_bf16.shape[1]
    p = n // 3
    nb = p // COL_TILE
    head_norm = head_gains is not None
    in_specs = [
        pl.BlockSpec((tm, d), lambda i, j: (i, 0)),
        pl.BlockSpec((1, d), lambda i, j: (0, 0)),
        pl.BlockSpec((d, COL_TILE), lambda i, j: (0, j)),
    ]
    args = [x, g.reshape(1, d), w_bf16]
    if head_norm:
        gq, gk = head_gains
        reps = COL_TILE // HEAD_DIM
        head_of = jnp.arange(MXU_DIM) // HEAD_DIM
        bd = jnp.where(head_of[:, None] == head_of[None, :], 1.0 / HEAD_DIM, 0.0).astype(BF16)
        in_specs += [
            pl.BlockSpec((1, COL_TILE), lambda i, j: (0, 0)),
            pl.BlockSpec((1, COL_TILE), lambda i, j: (0, 0)),
            pl.BlockSpec((MXU_DIM, MXU_DIM), lambda i, j: (0, 0)),
        ]
        args += [jnp.tile(gq.astype(F32), reps).reshape(1, COL_TILE),
                 jnp.tile(gk.astype(F32), reps).reshape(1, COL_TILE), bd]

    def part_map(first):
        return lambda i, j: (i, jnp.clip(j - first, 0, nb - 1))

    return pl.pallas_call(
        functools.partial(_qkv_kernel, nb=nb, head_norm=head_norm),
        out_shape=(jax.ShapeDtypeStruct((t, p), F32),) * 3,
        grid=(t // tm, 3 * nb),
        in_specs=in_specs,
        out_specs=(pl.BlockSpec((tm, COL_TILE), part_map(0)),
                   pl.BlockSpec((tm, COL_TILE), part_map(nb)),
                   pl.BlockSpec((tm, COL_TILE), part_map(2 * nb))),
        scratch_shapes=[pltpu.VMEM((tm, d), BF16)],
        compiler_params=_params("arbitrary", "arbitrary"),
        name="qkv_proj",
    )(*args)


def _qkv_t_kernel(x_ref, g_ref, wt_ref, q_ref, kt_ref, vt_ref, xn_ref):
    j = pl.program_id(1)

    @pl.when(j == 0)
    def _():
        xn_ref[...] = _rms(x_ref[...], g_ref[...]).astype(BF16)
        q_ref[...] = (_dot_nt(xn_ref[...], wt_ref[...]) * Q_SCALE).astype(q_ref.dtype)

    @pl.when(j == 1)
    def _():
        kt_ref[...] = _dot_nt(wt_ref[...], xn_ref[...])

    @pl.when(j == 2)
    def _():
        vt_ref[...] = _dot_nt(wt_ref[...], xn_ref[...])


def _qkv_proj_t(x, g, wt_bf16, *, batch, seq, tm):
    t, d = x.shape
    p = wt_bf16.shape[0] // 3
    tiles_per_seq = seq // tm
    kv_map = lambda i, j: (i // tiles_per_seq, i % tiles_per_seq)
    return pl.pallas_call(
        _qkv_t_kernel,
        out_shape=(jax.ShapeDtypeStruct((t, p), BF16),
                   jax.ShapeDtypeStruct((batch * p, seq), F32),
                   jax.ShapeDtypeStruct((batch * p, seq), F32)),
        grid=(t // tm, 3),
        in_specs=[pl.BlockSpec((tm, d), lambda i, j: (i, 0)),
                  pl.BlockSpec((1, d), lambda i, j: (0, 0)),
                  pl.BlockSpec((p, d), lambda i, j: (j, 0))],
        out_specs=(pl.BlockSpec((tm, p), lambda i, j: (i, 0)),
                   pl.BlockSpec((p, tm), kv_map),
                   pl.BlockSpec((p, tm), kv_map)),
        scratch_shapes=[pltpu.VMEM((tm, d), BF16)],
        compiler_params=_params("arbitrary", "arbitrary"),
        name="qkv_proj_t",
    )(x, g.reshape(1, d), wt_bf16)


def _kv_state_kernel(x_ref, g_ref, wk_ref, wv_ref, gk_ref, bd_ref, o_ref):
    p = wk_ref.shape[0]
    xn = _rms(x_ref[...], g_ref[...]).astype(BF16)
    kt = _dot_nt(wk_ref[...], xn)
    for c in range(p // MXU_DIM):
        rows = slice(c * MXU_DIM, (c + 1) * MXU_DIM)
        kc = kt[rows]
        sq = kc * kc
        hi = sq.astype(BF16)
        lo = (sq - hi.astype(F32)).astype(BF16)
        ms = _dot(bd_ref[...], hi) + _dot(bd_ref[...], lo)
        o_ref[rows, :] = (kc * lax.rsqrt(ms + NORM_EPS)) * gk_ref[rows, :]
    o_ref[p:, :] = _dot_nt(wv_ref[...], xn)


def _kv_state_t(x, g, wt_bf16, gk, *, group, n_groups, batch, seq, keep):
    t, d = x.shape
    p = wt_bf16.shape[0] // (3 * n_groups)
    tm = min(keep, ROW_TILE)
    tiles = keep // tm
    first = (seq - keep) // tm
    per_seq = seq // tm
    head_of = jnp.arange(MXU_DIM) // HEAD_DIM
    bd = jnp.where(head_of[:, None] == head_of[None, :], 1.0 / HEAD_DIM, 0.0).astype(BF16)
    gk_col = jnp.tile(gk.astype(F32), p // HEAD_DIM).reshape(p, 1)
    return pl.pallas_call(
        _kv_state_kernel,
        out_shape=jax.ShapeDtypeStruct((batch * 2 * p, keep), F32),
        grid=(batch, tiles),
        in_specs=[pl.BlockSpec((tm, d), lambda b, i: (b * per_seq + first + i, 0)),
                  pl.BlockSpec((1, d), lambda b, i: (0, 0)),
                  pl.BlockSpec((p, d), lambda b, i: (n_groups + group, 0)),
                  pl.BlockSpec((p, d), lambda b, i: (2 * n_groups + group, 0)),
                  pl.BlockSpec((p, 1), lambda b, i: (0, 0)),
                  pl.BlockSpec((MXU_DIM, MXU_DIM), lambda b, i: (0, 0))],
        out_specs=pl.BlockSpec((2 * p, tm), lambda b, i: (b, i)),
        compiler_params=_params("arbitrary", "arbitrary"),
        name="dsw_kv_state",
    )(x, g.reshape(1, d), wt_bf16, wt_bf16, gk_col, bd)


def _out_proj_kernel(a_ref, w_ref, res_ref, o_ref):
    o_ref[...] = res_ref[...] + _dot(a_ref[...].astype(BF16), w_ref[...])


def _out_proj(a, w_bf16, res, *, tm):
    t, d_in = a.shape
    d = w_bf16.shape[1]
    return pl.pallas_call(
        _out_proj_kernel,
        out_shape=jax.ShapeDtypeStruct((t, d), F32),
        grid=(t // tm,),
        in_specs=[pl.BlockSpec((tm, d_in), lambda i: (i, 0)),
                  pl.BlockSpec((d_in, d), lambda i: (0, 0)),
                  pl.BlockSpec((tm, d), lambda i: (i, 0))],
        out_specs=pl.BlockSpec((tm, d), lambda i: (i, 0)),
        compiler_params=_params("arbitrary"),
        name="out_proj",
    )(a, w_bf16, res)


def _softplus_parts(z):
    sp = jnp.maximum(z, 0.0) + jnp.log(1.0 + jnp.exp(-jnp.abs(z)))
    return sp, z - sp


def _sb_prompt_kernel(bias_ref, q_ref, kt_ref, vt_ref, tri_ref, o_ref, kb_ref, vm_ref, acc_ref, *, tq, tk):
    hq = pl.program_id(1)
    qi = pl.program_id(2)
    nh = SB_HEADS_PER_BLOCK
    s_len = kt_ref.shape[1]
    head_of_lane = lax.broadcasted_iota(I32, (1, MXU_DIM), 1) // HEAD_DIM
    head_of_row = lax.broadcasted_iota(I32, (MXU_DIM, 1), 0) // HEAD_DIM

    @pl.when(qi == 0)
    def _():
        for j in range(s_len // tk):
            cols = slice(j * tk, (j + 1) * tk)
            kb_ref[j] = kt_ref[:, cols].astype(BF16)
            vj = vt_ref[:, cols]
            vm_ref[j] = jnp.concatenate(
                [jnp.where(head_of_row == h, vj, 0.0) for h in range(nh)], axis=1).astype(BF16)

    q = q_ref[...]
    qm = jnp.concatenate([jnp.where(head_of_lane == h, q, jnp.zeros_like(q)) for h in range(nh)], axis=0)
    bias = jnp.concatenate([jnp.full((tq, 1), bias_ref[hq * nh + h], F32) for h in range(nh)], axis=0)
    tri = tri_ref[...]
    acc_ref[...] = jnp.zeros_like(acc_ref)

    def step(j, carry, mask):
        z = _dot(qm, kb_ref[j]) + bias
        sp, ls = _softplus_parts(z)
        if mask is not None:
            sp = jnp.where(mask, sp, 0.0)
        after = _dot(sp.astype(BF16), tri) + carry
        w = jnp.exp(ls - after)
        if mask is not None:
            w = jnp.where(mask, w, 0.0)
        wb = w.astype(BF16)
        wcat = jnp.concatenate([wb[h * tq:(h + 1) * tq] for h in range(nh)], axis=1)
        acc_ref[...] += _dot_nt(wcat, vm_ref[j])
        return carry + jnp.sum(sp, axis=1, keepdims=True)

    kb = (qi * tq) // tk
    row = lax.broadcasted_iota(I32, (nh * tq, tk), 0) % tq + (qi * tq - kb * tk)
    col = lax.broadcasted_iota(I32, (nh * tq, tk), 1)
    carry = jnp.zeros((nh * tq, 1), F32)
    for m in reversed(range(max(tq // tk, 1))):
        carry = step(kb + m, carry, col + m * tk < row)
    lax.fori_loop(0, kb, lambda t, c: step(kb - 1 - t, c, None), carry)
    o_ref[...] = acc_ref[...].astype(o_ref.dtype)


def _sb_prompt_attention(q, kt, vt, bias, *, batch, seq):
    width = q.shape[1]
    blocks_per_batch = width // MXU_DIM
    tq, tk = SB_Q_TILE, SB_K_TILE
    nq = seq // tq
    idx = jnp.arange(tk)
    tri = (idx[:, None] > idx[None, :]).astype(BF16)
    return pl.pallas_call(
        functools.partial(_sb_prompt_kernel, tq=tq, tk=tk),
        out_shape=jax.ShapeDtypeStruct((batch * seq, width), BF16),
        grid=(batch, width // MXU_DIM, nq),
        in_specs=[pl.BlockSpec(memory_space=pltpu.SMEM),
                  pl.BlockSpec((tq, MXU_DIM), lambda b, h, i: (b * nq + i, h)),
                  pl.BlockSpec((MXU_DIM, seq), lambda b, h, i: (b * blocks_per_batch + h, 0)),
                  pl.BlockSpec((MXU_DIM, seq), lambda b, h, i: (b * blocks_per_batch + h, 0)),
                  pl.BlockSpec((tk, tk), lambda b, h, i: (0, 0))],
        out_specs=pl.BlockSpec((tq, MXU_DIM), lambda b, h, i: (b * nq + i, h)),
        scratch_shapes=[pltpu.VMEM((seq // tk, MXU_DIM, tk), BF16),
                        pltpu.VMEM((seq // tk, MXU_DIM, SB_HEADS_PER_BLOCK * tk), BF16),
                        pltpu.VMEM((tq, MXU_DIM), F32)],
        compiler_params=_params("arbitrary", "arbitrary", "arbitrary"),
        name="sb_prompt_attention",
    )(bias.astype(F32), q, kt, vt, tri)


def _head_rows(x_row, n_heads):
    width = x_row.shape[1]
    lane_head = lax.broadcasted_iota(I32, (n_heads, width), 1) // HEAD_DIM
    row = lax.broadcasted_iota(I32, (n_heads, width), 0)
    return jnp.where(lane_head == row, jnp.broadcast_to(x_row, (n_heads, width)), 0.0)


def _head_diag(acc):
    n_heads, width = acc.shape
    lane_head = lax.broadcasted_iota(I32, (n_heads, width), 1) // HEAD_DIM
    row = lax.broadcasted_iota(I32, (n_heads, width), 0)
    return jnp.sum(jnp.where(lane_head == row, acc, 0.0), axis=0, keepdims=True)


def _sb_sample_kernel(pt_ref, q_ref, bias_ref, tri_ref, *refs, n_pages_step, n_heads):
    del pt_ref
    k_refs = refs[:n_pages_step]
    v_refs = refs[n_pages_step:2 * n_pages_step]
    o_ref, acc_ref, carry_ref = refs[2 * n_pages_step:]
    s = pl.program_id(1)

    @pl.when(s == 0)
    def _():
        acc_ref[...] = jnp.zeros_like(acc_ref)
        carry_ref[...] = jnp.zeros_like(carry_ref)

    qh = _head_rows(q_ref[0], n_heads).astype(BF16)
    kcat = jnp.concatenate([r[...].astype(BF16) for r in k_refs], axis=1)
    vcat = jnp.concatenate([r[...].astype(BF16) for r in v_refs], axis=1)
    z = _dot(qh, kcat) + bias_ref[...]
    sp, ls = _softplus_parts(z)
    sub = tri_ref.shape[0]
    carry = carry_ref[...]
    afters = []
    for c in reversed(range(z.shape[1] // sub)):
        sp_c = sp[:, c * sub:(c + 1) * sub]
        sp_hi = sp_c.astype(BF16)
        sp_lo = (sp_c - sp_hi.astype(F32)).astype(BF16)
        afters.append(_dot(sp_hi, tri_ref[...]) + _dot(sp_lo, tri_ref[...]) + carry)
        carry = carry + jnp.sum(sp_c, axis=1, keepdims=True)
    after = jnp.concatenate(afters[::-1], axis=1)
    w = jnp.exp(ls - after)
    acc_ref[...] += _dot_nt(w.astype(BF16), vcat)
    carry_ref[...] = carry

    @pl.when(s == pl.num_programs(1) - 1)
    def _():
        o_ref[0] = _head_diag(acc_ref[...])


def _sb_sample_attention(q, cache_k, cache_v, page_table, bias):
    bd, width = q.shape
    n_heads = width // HEAD_DIM
    page = cache_k.shape[2]
    n_pages = page_table.shape[1]
    pps = SB_PAGES_PER_STEP
    n_steps = n_pages // pps
    idx = jnp.arange(SB_SAMPLE_SUB_BLOCK)
    tri = (idx[:, None] > idx[None, :]).astype(BF16)
    assert (pps * page) % SB_SAMPLE_SUB_BLOCK == 0

    def page_map(u):
        return lambda b, s, pt: (pt[b * n_pages + (n_steps - 1 - s) * pps + u], 0, 0)

    kv_specs = [pl.BlockSpec((None, width, page), page_map(u)) for u in range(pps)]
    out = pl.pallas_call(
        functools.partial(_sb_sample_kernel, n_pages_step=pps, n_heads=n_heads),
        out_shape=jax.ShapeDtypeStruct((bd, 1, width), F32),
        grid_spec=pltpu.PrefetchScalarGridSpec(
            num_scalar_prefetch=1,
            grid=(bd, n_steps),
            in_specs=[pl.BlockSpec((1, 1, width), lambda b, s, pt: (b, 0, 0)),
                      pl.BlockSpec((n_heads, 1), lambda b, s, pt: (0, 0)),
                      pl.BlockSpec((SB_SAMPLE_SUB_BLOCK, SB_SAMPLE_SUB_BLOCK), lambda b, s, pt: (0, 0))]
            + kv_specs + kv_specs,
            out_specs=pl.BlockSpec((1, 1, width), lambda b, s, pt: (b, 0, 0)),
            scratch_shapes=[pltpu.VMEM((n_heads, width), F32), pltpu.VMEM((n_heads, 1), F32)],
        ),
        compiler_params=_params("arbitrary", "arbitrary"),
        name="sb_sample_attention",
    )(page_table.reshape(-1).astype(I32), q.astype(F32).reshape(bd, 1, width),
      bias.astype(F32).reshape(n_heads, 1), tri, *([cache_k] * pps), *([cache_v] * pps))
    return out.reshape(bd, width)


def _dsw_prompt_kernel(slope_ref, *refs, seq):
    q_refs, k_refs, v_refs = refs[0:3], refs[3:6], refs[6:9]
    o_ref, og_ref, lg_ref = refs[9:]
    hp = pl.program_id(1)
    blk_rows = DSW_BLOCK
    n_groups = len(DSW_DILATIONS)
    heads_per_group = slope_ref.shape[0] // n_groups
    lane = lax.broadcasted_iota(I32, (1, LANES), 1)
    first_head = lane < HEAD_DIM

    qrow = lax.broadcasted_iota(I32, (2 * blk_rows, 2 * blk_rows), 0) % blk_rows
    kcol = lax.broadcasted_iota(I32, (2 * blk_rows, 2 * blk_rows), 1)
    steps = qrow + blk_rows - kcol
    in_band = (steps >= 0) & (steps <= DSW_SPAN)
    steps_f = steps.astype(F32)
    upper = lax.broadcasted_iota(I32, (2 * blk_rows, 1), 0) >= blk_rows

    for g, dil in enumerate(DSW_DILATIONS):
        n_blk = seq // dil // blk_rows
        s0 = slope_ref[g * heads_per_group + 2 * hp]
        s1 = slope_ref[g * heads_per_group + 2 * hp + 1]
        penalty = jnp.where(upper, s1, s0) * steps_f
        q_ref, k_ref, v_ref = q_refs[g], k_refs[g], v_refs[g]

        def rows_at(first, dil=dil):
            if dil == 1:
                return pl.ds(pl.multiple_of(first, blk_rows), blk_rows)
            return pl.ds(first, blk_rows, stride=dil)

        def block(t, _, g=g, dil=dil, n_blk=n_blk, penalty=penalty, q_ref=q_ref, k_ref=k_ref,
                  v_ref=v_ref, rows_at=rows_at):
            res = t // n_blk
            blk = t % n_blk
            own = res + dil * blk * blk_rows
            prev = res + dil * jnp.maximum(blk - 1, 0) * blk_rows
            qb = q_ref[rows_at(own), :]
            kc = jnp.concatenate([k_ref[rows_at(prev), :], k_ref[rows_at(own), :]], axis=0)
            vc = jnp.concatenate([v_ref[rows_at(prev), :], v_ref[rows_at(own), :]], axis=0)
            qm = jnp.concatenate([jnp.where(first_head, qb, 0.0), jnp.where(first_head, 0.0, qb)], axis=0)
            sc = _dot_nt(qm.astype(BF16), kc.astype(BF16)) - penalty
            valid = in_band & ((kcol >= blk_rows) | (blk > 0))
            sc = jnp.where(valid, sc, NEG_BIG)
            m = jnp.max(sc, axis=1, keepdims=True)
            p = jnp.exp(sc - m)
            l = jnp.sum(p, axis=1, keepdims=True)
            pb = p.astype(BF16)
            pcat = jnp.concatenate([pb[:blk_rows], pb[blk_rows:]], axis=1)
            vm = jnp.concatenate([jnp.where(first_head, vc, 0.0), jnp.where(first_head, 0.0, vc)], axis=0)
            o = _dot(pcat, vm.astype(BF16))
            lse = m + jnp.log(l)
            l_b = jnp.where(first_head, l[:blk_rows], l[blk_rows:])
            lse_b = jnp.where(first_head, lse[:blk_rows], lse[blk_rows:])
            og_ref[g, rows_at(own), :] = o / l_b
            lg_ref[g, rows_at(own), :] = lse_b
            return 0

        n_iter = dil * n_blk // DSW_BLOCKS_PER_ITER
        assert n_iter * DSW_BLOCKS_PER_ITER == dil * n_blk

        def blocks(it, carry, block=block):
            for u in range(DSW_BLOCKS_PER_ITER):
                block(it * DSW_BLOCKS_PER_ITER + u, carry)
            return carry

        lax.fori_loop(0, n_iter, blocks, 0)

    chunk = 2 * blk_rows
    for c in range(seq // chunk):
        rows = pl.ds(c * chunk, chunk)
        ls = [lg_ref[g, rows, :] for g in range(n_groups)]
        m = functools.reduce(jnp.maximum, ls)
        es = [jnp.exp(x - m) for x in ls]
        num = sum(e * og_ref[g, rows, :] for g, e in enumerate(es))
        o_ref[rows, :] = (num / sum(es)).astype(o_ref.dtype)


def _dsw_prompt_attention(q, k, v, slopes, *, batch, seq):
    n_groups = len(DSW_DILATIONS)
    width = q.shape[1] // n_groups
    blocks_per_group = width // LANES

    def col_map(g):
        return lambda b, h: (b, g * blocks_per_group + h)

    specs = [pl.BlockSpec((seq, LANES), col_map(g)) for g in range(n_groups)]
    return pl.pallas_call(
        functools.partial(_dsw_prompt_kernel, seq=seq),
        out_shape=jax.ShapeDtypeStruct((batch * seq, width), BF16),
        grid=(batch, blocks_per_group),
        in_specs=[pl.BlockSpec(memory_space=pltpu.SMEM)] + specs * 3,
        out_specs=pl.BlockSpec((seq, LANES), lambda b, h: (b, h)),
        scratch_shapes=[pltpu.VMEM((n_groups, seq, LANES), F32), pltpu.VMEM((n_groups, seq, LANES), F32)],
        compiler_params=_params("arbitrary", "arbitrary"),
        name="dsw_prompt_attention",
    )(slopes, *([q] * n_groups), *([k] * n_groups), *([v] * n_groups))


def _dsw_sample_kernel(slope_ref, q_ref, kn_ref, vn_ref, *refs, rows, heads_per_block):
    n_groups = len(rows)
    buf_refs = refs[:n_groups]
    o_ref = refs[n_groups]
    out_refs = refs[n_groups + 1:2 * n_groups + 1]
    p_refs = refs[2 * n_groups + 1:3 * n_groups + 1]
    pn_ref, alpha_ref = refs[3 * n_groups + 1:]
    j = pl.program_id(1)
    nq = pl.num_programs(1) // 2
    hb = heads_per_block
    width = hb * HEAD_DIM

    def column(row):
        eye = lax.broadcasted_iota(I32, (width, width), 0) == lax.broadcasted_iota(I32, (width, width), 1)
        return jnp.sum(jnp.where(eye, jnp.broadcast_to(row, (width, width)), 0.0), axis=1, keepdims=True)

    def moved_down(x, new_row):
        n = x.shape[1]
        lane = lax.broadcasted_iota(I32, (1, n), 1)
        return jnp.where(lane == n - 1, column(new_row), pltpu.roll(x, n - 1, 1))

    def rounded(x):
        return x.astype(BF16).astype(F32)

    @pl.when(j < nq)
    def _():
        lses, pns = [], []
        for g, dil in enumerate(DSW_DILATIONS):
            kt = buf_refs[g][...]
            qh = _head_rows(q_ref[0, pl.ds(g * nq + j, 1), :], hb)
            k_new = kn_ref[0, pl.ds(g * nq + j, 1), :]
            row = lax.broadcasted_iota(I32, (1, rows[g]), 1)
            back = (rows[g] - row).astype(F32)
            sc = _dot(qh.astype(BF16), kt.astype(BF16)) - slope_ref[g] * back
            sc = jnp.where((row & (dil - 1)) == 0, sc, NEG_BIG)
            sn = jnp.sum(rounded(qh) * rounded(k_new), axis=1, keepdims=True)
            m = jnp.maximum(jnp.max(sc, axis=1, keepdims=True), sn)
            lse = m + jnp.log(jnp.sum(jnp.exp(sc - m), axis=1, keepdims=True) + jnp.exp(sn - m))
            p_refs[g][j] = jnp.exp(sc - lse)
            pns.append(jnp.exp(sn - lse))
            lses.append(lse)
            out_refs[g][...] = moved_down(kt, k_new)
        m = functools.reduce(jnp.maximum, lses)
        es = [jnp.exp(l - m) for l in lses]
        pn_ref[j] = jnp.concatenate(pns, axis=1)
        alpha_ref[j] = jnp.concatenate(es, axis=1) / sum(es)

    @pl.when(j >= nq)
    def _():
        jq = j - nq
        pn = rounded(pn_ref[jq])
        alpha = rounded(alpha_ref[jq])
        acc = jnp.zeros((hb, width), F32)
        for g in range(n_groups):
            vt = buf_refs[g][...]
            v_new = vn_ref[0, pl.ds(g * nq + jq, 1), :]
            o_g = _dot_nt(p_refs[g][jq].astype(BF16), vt.astype(BF16)) + pn[:, g:g + 1] * rounded(v_new)
            acc = acc + alpha[:, g:g + 1] * rounded(o_g)
            out_refs[g][...] = moved_down(vt, v_new)
        o_ref[0] = _head_diag(acc)


def _dsw_sample_attention(q, k_new, v_new, bufs_t, slopes):
    bd, total = q.shape
    n_groups = len(DSW_DILATIONS)
    width = total // n_groups
    rows = tuple(b.shape[2] for b in bufs_t)
    for g, dil in enumerate(DSW_DILATIONS):
        assert rows[g] == DSW_SPAN * dil
    hb = DSW_SAMPLE_HEADS
    block = hb * HEAD_DIM
    nq = width // block
    blocked = lambda a: a.astype(F32).reshape(bd, n_groups * nq, block)
    vec_spec = pl.BlockSpec((1, n_groups * nq, block), lambda b, j: (b, 0, 0))
    buf_specs = [pl.BlockSpec((None, block, r), lambda b, j: (b, j, 0)) for r in rows]
    outs = pl.pallas_call(
        functools.partial(_dsw_sample_kernel, rows=rows, heads_per_block=hb),
        out_shape=(jax.ShapeDtypeStruct((bd * nq, 1, block), F32),)
        + tuple(jax.ShapeDtypeStruct(b.shape, F32) for b in bufs_t),
        grid=(bd, 2 * nq),
        in_specs=[pl.BlockSpec((n_groups, None, hb, 1), lambda b, j: (0, jnp.minimum(j, nq - 1), 0, 0)),
                  vec_spec, vec_spec, vec_spec] + buf_specs,
        out_specs=(pl.BlockSpec((1, 1, block), lambda b, j: (b * nq + jnp.maximum(j - nq, 0), 0, 0)),)
        + tuple(buf_specs),
        scratch_shapes=[pltpu.VMEM((nq, hb, r), F32) for r in rows]
        + [pltpu.VMEM((nq, hb, n_groups), F32), pltpu.VMEM((nq, hb, n_groups), F32)],
        compiler_params=_params("arbitrary", "arbitrary"),
        name="dsw_sample_attention",
    )(slopes.reshape(n_groups, nq, hb, 1), blocked(q), blocked(k_new), blocked(v_new), *bufs_t)
    return outs[0].reshape(bd, width), outs[1:]


def _router_kernel(hp_ref, hs_ref, g_ref, w_ref, b_ref, xn_ref, eid_ref, gate_ref, *, n_groups, n_experts):
    i = pl.program_id(0)
    n_prompt_tiles = pl.num_programs(0) - 1
    lane = lax.broadcasted_iota(I32, (1, LANES), 1)
    lane_f = lane.astype(F32)
    big = float(LANES)

    def first_lane(hit):
        return jnp.min(jnp.where(hit, lane_f, big), axis=1, keepdims=True).astype(I32)

    def route(x):
        xn = _rms(x, g_ref[...])
        logits = _dot(xn.astype(BF16), w_ref[...]) + b_ref[...]
        gl = jnp.where(lane < n_groups, logits, -jnp.inf)
        gmax = jnp.max(gl, axis=1, keepdims=True)
        gsel = first_lane(gl == gmax)
        gden = jnp.sum(jnp.exp(gl - gmax), axis=1, keepdims=True)
        first = n_groups + gsel * n_experts
        el = jnp.where((lane >= first) & (lane < first + n_experts), logits, -jnp.inf)
        m1 = jnp.max(el, axis=1, keepdims=True)
        i1 = first_lane(el == m1)
        el2 = jnp.where(lane == i1, -jnp.inf, el)
        m2 = jnp.max(el2, axis=1, keepdims=True)
        i2 = first_lane(el2 == m2)
        e2 = jnp.exp(m2 - m1)
        w1 = 1.0 / (1.0 + e2) / gden
        w2 = e2 / (1.0 + e2) / gden
        eid = jnp.where(lane == 0, i1 - n_groups, jnp.where(lane == 1, i2 - n_groups, 0))
        gate = jnp.where(lane == 0, w1, jnp.where(lane == 1, w2, 0.0))
        return xn, eid, gate

    def store_token_major(xn):
        for s in range(xn.shape[1] // LANES):
            xn_ref[pl.ds(s, xn.shape[0], stride=TOKEN_ROWS), :] = xn[:, s * LANES:(s + 1) * LANES]

    @pl.when(i < n_prompt_tiles)
    def _():
        xn, eid, gate = route(hp_ref[...])
        store_token_major(xn)
        eid_ref[...] = eid
        gate_ref[...] = gate

    @pl.when(i == n_prompt_tiles)
    def _():
        ns = hs_ref.shape[0]
        xn, eid, gate = route(hs_ref[...])
        xn_ref[...] = jnp.zeros_like(xn_ref)
        eid_ref[...] = jnp.zeros_like(eid_ref)
        gate_ref[...] = jnp.zeros_like(gate_ref)
        store_token_major(xn)
        eid_ref[0:ns, :] = eid
        gate_ref[0:ns, :] = gate


def _router(h_prompt, h_sample, g, w_router, b_router, *, n_groups, n_experts, tm):
    tp, d = h_prompt.shape
    assert d == TOKEN_ROWS * LANES
    ns = h_sample.shape[0]
    n_tiles = tp // tm + 1
    last = tp // tm - 1
    return pl.pallas_call(
        functools.partial(_router_kernel, n_groups=n_groups, n_experts=n_experts),
        out_shape=(jax.ShapeDtypeStruct((n_tiles * tm * TOKEN_ROWS, LANES), F32),
                   jax.ShapeDtypeStruct((n_tiles * tm, LANES), I32),
                   jax.ShapeDtypeStruct((n_tiles * tm, LANES), F32)),
        grid=(n_tiles,),
        in_specs=[pl.BlockSpec((tm, d), lambda i: (jnp.minimum(i, last), 0)),
                  pl.BlockSpec((ns, d), lambda i: (0, 0)),
                  pl.BlockSpec((1, d), lambda i: (0, 0)),
                  pl.BlockSpec((d, LANES), lambda i: (0, 0)),
                  pl.BlockSpec((1, LANES), lambda i: (0, 0))],
        out_specs=(pl.BlockSpec((tm * TOKEN_ROWS, LANES), lambda i: (i, 0)),
                   pl.BlockSpec((tm, LANES), lambda i: (i, 0)),
                   pl.BlockSpec((tm, LANES), lambda i: (i, 0))),
        compiler_params=_params("arbitrary"),
        name="moe_router",
    )(h_prompt, h_sample, g.reshape(1, d), w_router, b_router)


def _moe_kernel(tile_e_ref, nvalid_ref, rows_ref, slot_ref, slot_next_ref, gate_ref, x_hbm,
                wg_ref, wu_ref, wd_ref, y_hbm, xbuf, ybuf, wgb, wub, wdb, gsem, ssem):
    i = pl.program_id(0)
    nv = nvalid_ref[0]
    tm = xbuf.shape[1] // TOKEN_ROWS
    n_chunks = TOKEN_ROWS
    slot = lax.rem(i, 2)

    def tile_rows(first):
        return pl.ds(pl.multiple_of(first * TOKEN_ROWS, TOKEN_ROWS), TOKEN_ROWS)

    def gather(idx_ref, s):
        def issue(r, _):
            pltpu.make_async_copy(x_hbm.at[tile_rows(idx_ref[0, 0, r] >> 1), :], xbuf.at[s, tile_rows(r), :],
                                  gsem.at[s]).start()
            return 0
        lax.fori_loop(0, tm, issue, 0, unroll=8)

    def gather_done(s):
        return pltpu.make_async_copy(x_hbm.at[pl.ds(0, tm * TOKEN_ROWS), :], xbuf.at[s], gsem.at[s])

    def scatter_wait(s, tile):
        n = pl.multiple_of(rows_ref[tile] * TOKEN_ROWS, TOKEN_ROWS)
        pltpu.make_async_copy(ybuf.at[s, pl.ds(0, n), :], y_hbm.at[pl.ds(0, n), :], ssem.at[s]).wait()

    @pl.when(i == 0)
    def _():
        gather(slot_ref, 0)

    @pl.when(i < nv)
    def _():
        gather_done(slot).wait()

        @pl.when(i + 1 < nv)
        def _():
            gather(slot_next_ref, 1 - slot)

        @pl.when((i == 0) | (tile_e_ref[i] != tile_e_ref[jnp.maximum(i - 1, 0)]))
        def _():
            wgb[...] = wg_ref[...].astype(BF16)
            wub[...] = wu_ref[...].astype(BF16)
            wdb[...] = wd_ref[...].astype(BF16)

        x = jnp.concatenate([xbuf[slot, pl.ds(c, tm, stride=TOKEN_ROWS), :] for c in range(n_chunks)],
                            axis=1).astype(BF16)
        hg = _dot(x, wgb[...])
        hu = _dot(x, wub[...])
        h = (hg / (1.0 + jnp.exp(-hg))) * hu * gate_ref[...]
        y = _dot(h.astype(BF16), wdb[...])

        @pl.when(i >= 2)
        def _():
            scatter_wait(slot, i - 2)

        for c in range(n_chunks):
            ybuf[slot, pl.ds(c, tm, stride=TOKEN_ROWS), :] = y[:, c * LANES:(c + 1) * LANES]
        n_rows = rows_ref[i]

        def issue(r, _):
            @pl.when(r < n_rows)
            def _():
                pltpu.make_async_copy(ybuf.at[slot, tile_rows(r), :], y_hbm.at[tile_rows(slot_ref[0, 0, r]), :],
                                      ssem.at[slot]).start()
            return 0
        lax.fori_loop(0, tm, issue, 0, unroll=8)

        @pl.when(i == nv - 1)
        def _():
            scatter_wait(slot, i)

            @pl.when(i >= 1)
            def _():
                scatter_wait(1 - slot, i - 1)


def _moe_experts(xn_all, eid, gate, w_gate, w_up, w_down, *, n_real, first_expert, n_exp):
    assert MOE_TOP_K == 2
    _, d, d_ff = w_gate.shape
    tm = MOE_TILE
    n_slots = n_real * MOE_TOP_K
    n_tiles = -(-n_slots // tm) + n_exp
    n_pos = n_tiles * tm

    e_flat = eid.reshape(-1)
    tok_flat = jnp.repeat(jnp.arange(n_real, dtype=I32), MOE_TOP_K)
    k_flat = jnp.tile(jnp.arange(MOE_TOP_K, dtype=I32), n_real)
    onehot = (e_flat[:, None] == jnp.arange(n_exp, dtype=I32)[None, :]).astype(I32)
    csum = jnp.cumsum(onehot, axis=0)
    rank = jnp.sum((csum - 1) * onehot, axis=1)
    counts = csum[-1]
    tiles_e = (counts + tm - 1) // tm
    tiles_end = jnp.cumsum(tiles_e)
    tile_start = tiles_end - tiles_e
    pos = jnp.sum(onehot * tile_start[None, :], axis=1) * tm + rank
    n_valid = tiles_end[-1]
    tile_ids = jnp.arange(n_tiles, dtype=I32)
    tile_e = jnp.minimum(jnp.sum((tiles_end[None, :] <= tile_ids[:, None]).astype(I32), axis=1), n_exp - 1)
    tile_first = jnp.sum((tile_e[:, None] == jnp.arange(n_exp, dtype=I32)[None, :]) * tile_start[None, :], axis=1)
    tile_count = jnp.sum((tile_e[:, None] == jnp.arange(n_exp, dtype=I32)[None, :]) * counts[None, :], axis=1)
    tile_rows = jnp.where(tile_ids < n_valid, jnp.clip(tile_count - (tile_ids - tile_first) * tm, 0, tm), 0).astype(I32)
    tile_e = jnp.where(tile_ids < n_valid, tile_e, tile_e[jnp.maximum(n_valid - 1, 0)]).astype(I32) + first_expert
    slot_flat = tok_flat * MOE_TOP_K + k_flat
    packed = jnp.stack([slot_flat, lax.bitcast_convert_type(gate.reshape(-1).astype(F32), I32)], axis=1)
    table = jnp.zeros((n_pos, 2), I32).at[pos].set(packed, unique_indices=True)
    slot3 = table[:, 0].reshape(n_tiles, 1, tm)
    gate_tbl = lax.bitcast_convert_type(table[:, 1], F32).reshape(n_pos, 1)

    smem_tile = lambda off: pl.BlockSpec(
        (1, 1, tm), lambda i, te, nv, tr: (jnp.minimum(i + off, n_tiles - 1), 0, 0), memory_space=pltpu.SMEM)
    expert_block = lambda shape: pl.BlockSpec((None,) + shape, lambda i, te, nv, tr: (te[i], 0, 0))
    return pl.pallas_call(
        _moe_kernel,
        out_shape=jax.ShapeDtypeStruct((n_slots * TOKEN_ROWS, LANES), F32),
        grid_spec=pltpu.PrefetchScalarGridSpec(
            num_scalar_prefetch=3,
            grid=(n_tiles,),
            in_specs=[smem_tile(0), smem_tile(1),
                      pl.BlockSpec((tm, 1), lambda i, te, nv, tr: (i, 0)),
                      pl.BlockSpec(memory_space=pl.ANY),
                      expert_block((d, d_ff)), expert_block((d, d_ff)), expert_block((d_ff, d))],
            out_specs=pl.BlockSpec(memory_space=pl.ANY),
            scratch_shapes=[pltpu.VMEM((2, tm * TOKEN_ROWS, LANES), F32), pltpu.VMEM((2, tm * TOKEN_ROWS, LANES), F32),
                            pltpu.VMEM((d, d_ff), BF16), pltpu.VMEM((d, d_ff), BF16), pltpu.VMEM((d_ff, d), BF16),
                            pltpu.SemaphoreType.DMA((2,)), pltpu.SemaphoreType.DMA((2,))],
        ),
        compiler_params=_params("arbitrary"),
        name="moe_experts",
    )(tile_e, n_valid.reshape(1).astype(I32), tile_rows, slot3, slot3, gate_tbl, xn_all, w_gate, w_up, w_down)


def _combine_kernel(h_ref, y_ref, o_ref):
    tm = h_ref.shape[0]
    stride = MOE_TOP_K * TOKEN_ROWS
    for c in range(TOKEN_ROWS):
        cols = slice(c * LANES, (c + 1) * LANES)
        o_ref[:, cols] = h_ref[:, cols] + (y_ref[pl.ds(c, tm, stride=stride), :]
                                           + y_ref[pl.ds(TOKEN_ROWS + c, tm, stride=stride), :])


def _combine(h, y_tiles, *, first_row, tm):
    t, d = h.shape
    b0 = first_row // tm
    rows = tm * MOE_TOP_K * TOKEN_ROWS
    return pl.pallas_call(
        _combine_kernel,
        out_shape=jax.ShapeDtypeStruct((t, d), F32),
        grid=(t // tm,),
        in_specs=[pl.BlockSpec((tm, d), lambda i: (i, 0)),
                  pl.BlockSpec((rows, LANES), lambda i: (b0 + i, 0))],
        out_specs=pl.BlockSpec((tm, d), lambda i: (i, 0)),
        compiler_params=_params("arbitrary"),
        name="moe_combine",
    )(h, y_tiles)


def _hier_moe(hp, hs, g, w_group, b_group, w_expert, b_expert, w_gate, w_up, w_down, layer):
    tp, d = hp.shape
    ns = hs.shape[0]
    n_groups, _, n_experts = w_expert.shape
    n_exp = n_groups * n_experts
    n_route = n_groups * (1 + n_experts)
    w_router = jnp.concatenate(
        [w_group, jnp.moveaxis(w_expert, 0, 1).reshape(d, n_groups * n_experts),
         jnp.zeros((d, LANES - n_route), F32)], axis=1).astype(BF16)
    b_router = jnp.concatenate(
        [b_group, b_expert.reshape(-1), jnp.zeros((LANES - n_route,), F32)]).astype(F32).reshape(1, LANES)
    xn_all, eid, gate = _router(hp, hs, g, w_router, b_router, n_groups=n_groups, n_experts=n_experts, tm=ROW_TILE)
    n_real = tp + ns
    d_ff = w_gate.shape[-1]
    y_tiles = _moe_experts(xn_all, eid[:n_real, :MOE_TOP_K], gate[:n_real, :MOE_TOP_K],
                           w_gate.reshape(-1, d, d_ff), w_up.reshape(-1, d, d_ff), w_down.reshape(-1, d_ff, d),
                           n_real=n_real, first_expert=layer * n_exp, n_exp=n_exp)
    return (_combine(hp, y_tiles, first_row=0, tm=ROW_TILE),
            _combine(hs, y_tiles, first_row=tp, tm=ns))


def kernel(x_prompt, x_sample, cache_sb_k, cache_sb_v, cache_dsw0_kv, cache_dsw1_kv, cache_dsw2_kv, page_table, norm1_g, norm2_g, sb_w_in, sb_logit_bias, sb_w_out, dsw_w_in, dsw_q_norm_g, dsw_k_norm_g, dsw_w_out, moe_w_group, moe_b_group, moe_w_expert, moe_b_expert, moe_w_gate, moe_w_up, moe_w_down):
    batch, seq, d = x_prompt.shape
    bd, dec_seq, _ = x_sample.shape
    assert dec_seq == 1
    depth = norm1_g.shape[0]
    dsw_caches = (cache_dsw0_kv, cache_dsw1_kv, cache_dsw2_kv)
    n_groups = len(DSW_DILATIONS)
    sb_heads = cache_sb_k.shape[3]
    dsw_heads = cache_dsw0_kv.shape[4]
    page = cache_sb_k.shape[2]
    tp = batch * seq

    hp = x_prompt.reshape(tp, d)
    hs = x_sample.reshape(bd, d)
    sb_kp, sb_vp, sb_ks, sb_vs = [], [], [], []
    dsw_p = [[] for _ in range(n_groups)]
    dsw_s = [[] for _ in range(n_groups)]

    for layer in range(depth):
        if layer % 2 == 0:
            la = layer // 2
            w_in = sb_w_in[la].astype(BF16)
            w_out = sb_w_out[la].astype(BF16)
            width = sb_heads * HEAD_DIM
            qp, kpt, vpt = _qkv_proj_t(hp, norm1_g[layer], w_in.T, batch=batch, seq=seq, tm=SB_QKV_ROW_TILE)
            qs, ks, vs = _qkv_proj(hs, norm1_g[layer], w_in, tm=bd)
            op = _sb_prompt_attention(qp, kpt, vpt, sb_logit_bias[la], batch=batch, seq=seq)
            pages_t = lambda c: jnp.transpose(c[:, la], (0, 2, 3, 1)).reshape(-1, width, page)
            os_ = _sb_sample_attention(qs, pages_t(cache_sb_k), pages_t(cache_sb_v), page_table, sb_logit_bias[la])
            hp = _out_proj(op, w_out, hp, tm=ROW_TILE)
            hs = _out_proj(os_, w_out, hs, tm=bd)
            seq_major = lambda a: jnp.transpose(a.reshape(batch, sb_heads, HEAD_DIM, seq), (0, 3, 1, 2))
            sb_kp.append(seq_major(kpt))
            sb_vp.append(seq_major(vpt))
            sb_ks.append(ks.reshape(bd, 1, sb_heads, HEAD_DIM))
            sb_vs.append(vs.reshape(bd, 1, sb_heads, HEAD_DIM))
        else:
            lb = layer // 2
            w_in = dsw_w_in[lb].astype(BF16)
            w_out = dsw_w_out[lb].astype(BF16)
            gains = (dsw_q_norm_g[lb], dsw_k_norm_g[lb])
            width = dsw_heads * HEAD_DIM
            n = jnp.arange(1, n_groups * dsw_heads + 1, dtype=F32)
            slopes = (2.0 ** (-8.0 * n / (n_groups * dsw_heads))).reshape(n_groups, dsw_heads)
            step_penalty = slopes * jnp.asarray(DSW_DILATIONS, F32)[:, None]
            qp, kp, vp = _qkv_proj(hp, norm1_g[layer], w_in, tm=DSW_QKV_ROW_TILE, head_gains=gains)
            qs, ks, vs = _qkv_proj(hs, norm1_g[layer], w_in, tm=bd, head_gains=gains)
            op = _dsw_prompt_attention(qp, kp, vp, step_penalty.reshape(-1), batch=batch, seq=seq)
            bufs_t = [jnp.transpose(c[lb], (0, 2, 3, 4, 1)).reshape(bd, 2 * width, c.shape[2]) for c in dsw_caches]
            os_, new_bufs_t = _dsw_sample_attention(qs, ks, vs, bufs_t, slopes)
            w_in_t = w_in.T
            for g in range(n_groups):
                keep = min(dsw_caches[g].shape[2], seq)
                state_t = _kv_state_t(hp, norm1_g[layer], w_in_t, gains[1], group=g, n_groups=n_groups,
                                      batch=batch, seq=seq, keep=keep)
                dsw_p[g].append(jnp.transpose(state_t.reshape(batch, 2, dsw_heads, HEAD_DIM, keep), (0, 4, 1, 2, 3)))
            hp = _out_proj(op, w_out, hp, tm=ROW_TILE)
            hs = _out_proj(os_, w_out, hs, tm=bd)
            for g in range(n_groups):
                rows_g = new_bufs_t[g].shape[2]
                dsw_s[g].append(jnp.transpose(new_bufs_t[g].reshape(bd, 2, dsw_heads, HEAD_DIM, rows_g),
                                              (0, 4, 1, 2, 3)))
        hp, hs = _hier_moe(hp, hs, norm2_g[layer], moe_w_group[layer], moe_b_group[layer], moe_w_expert[layer],
                           moe_b_expert[layer], moe_w_gate, moe_w_up, moe_w_down, layer)

    return (hp.reshape(batch, seq, d), hs.reshape(bd, 1, d),
            jnp.stack(sb_kp, axis=1), jnp.stack(sb_vp, axis=1),
            jnp.stack(sb_ks, axis=1), jnp.stack(sb_vs, axis=1),
            jnp.stack(dsw_p[0], axis=0), jnp.stack(dsw_s[0], axis=0),
            jnp.stack(dsw_p[1], axis=0), jnp.stack(dsw_s[1], axis=0),
            jnp.stack(dsw_p[2], axis=0), jnp.stack(dsw_s[2], axis=0))
```

```python
import functools

import jax
import jax.numpy as jnp
from jax import lax
from jax.experimental import pallas as pl
from jax.experimental.pallas import tpu as pltpu

F32 = jnp.float32
BF16 = jnp.bfloat16
I32 = jnp.int32

HEAD_DIM = 64
DSW_DILATIONS = (1, 4, 16)
DSW_SPAN = 128
MOE_TOP_K = 2
NORM_EPS = 1e-6
Q_SCALE = HEAD_DIM ** -0.5

LANES = 128
TOKEN_ROWS = 8
MXU_DIM = 256
VMEM_LIMIT = 48 * 1024 * 1024

NEG_BIG = -1e30

ROW_TILE = 512
COL_TILE = 1024
DSW_QKV_ROW_TILE = 1024
SB_QKV_ROW_TILE = 1024
MOE_TILE = 256
MOE_DMA_QUEUES = 2
SB_Q_TILE = 256
SB_K_TILE = 256
SB_HEADS_PER_BLOCK = MXU_DIM // HEAD_DIM
SB_PAGES_PER_STEP = 8
SB_SAMPLE_SUB_BLOCK = 512
DSW_BLOCK = 128
DSW_BLOCKS_PER_ITER = 8
DSW_SAMPLE_HEADS = MXU_DIM // HEAD_DIM


def _params(*semantics):
    return pltpu.CompilerParams(dimension_semantics=semantics, vmem_limit_bytes=VMEM_LIMIT)


def _rms(x, g):
    ms = jnp.mean(x * x, axis=-1, keepdims=True)
    return (x * lax.rsqrt(ms + NORM_EPS)) * g


def _dot(a, b):
    return jnp.dot(a, b, preferred_element_type=F32)


def _dot_nt(a, b):
    return lax.dot_general(a, b, (((1,), (1,)), ((), ())), preferred_element_type=F32)


def _qkv_kernel(*refs, nb, head_norm):
    if head_norm:
        x_ref, g_ref, w_ref, gq_ref, gk_ref, bd_ref, q_ref, k_ref, v_ref, xn_ref = refs
    else:
        x_ref, g_ref, w_ref, q_ref, k_ref, v_ref, xn_ref = refs
    j = pl.program_id(1)

    @pl.when(j == 0)
    def _():
        xn_ref[...] = _rms(x_ref[...], g_ref[...]).astype(BF16)

    y = _dot(xn_ref[...], w_ref[...])

    def head_normed(y, gain_ref):
        parts = []
        for c in range(y.shape[1] // MXU_DIM):
            yc = y[:, c * MXU_DIM:(c + 1) * MXU_DIM]
            sq = yc * yc
            hi = sq.astype(BF16)
            lo = (sq - hi.astype(F32)).astype(BF16)
            ms = _dot(hi, bd_ref[...]) + _dot(lo, bd_ref[...])
            parts.append(yc * lax.rsqrt(ms + NORM_EPS))
        return jnp.concatenate(parts, axis=1) * gain_ref[...]

    @pl.when(j < nb)
    def _():
        q = head_normed(y, gq_ref) if head_norm else y
        q_ref[...] = q * Q_SCALE

    @pl.when((j >= nb) & (j < 2 * nb))
    def _():
        k_ref[...] = head_normed(y, gk_ref) if head_norm else y

    @pl.when(j >= 2 * nb)
    def _():
        v_ref[...] = y


def _qkv_proj(x, g, w_bf16, *, tm, head_gains=None):
    t, d = x.shape
    n = w_bf16.shape[1]
    p = n // 3
    nb = p // COL_TILE
    head_norm = head_gains is not None
    in_specs = [
        pl.BlockSpec((tm, d), lambda i, j: (i, 0)),
        pl.BlockSpec((1, d), lambda i, j: (0, 0)),
        pl.BlockSpec((d, COL_TILE), lambda i, j: (0, j)),
    ]
    args = [x, g.reshape(1, d), w_bf16]
    if head_norm:
        gq, gk = head_gains
        reps = COL_TILE // HEAD_DIM
        head_of = jnp.arange(MXU_DIM) // HEAD_DIM
        bd = jnp.where(head_of[:, None] == head_of[None, :], 1.0 / HEAD_DIM, 0.0).astype(BF16)
        in_specs += [
            pl.BlockSpec((1, COL_TILE), lambda i, j: (0, 0)),
            pl.BlockSpec((1, COL_TILE), lambda i, j: (0, 0)),
            pl.BlockSpec((MXU_DIM, MXU_DIM), lambda i, j: (0, 0)),
        ]
        args += [jnp.tile(gq.astype(F32), reps).reshape(1, COL_TILE),
                 jnp.tile(gk.astype(F32), reps).reshape(1, COL_TILE), bd]

    def part_map(first):
        return lambda i, j: (i, jnp.clip(j - first, 0, nb - 1))

    return pl.pallas_call(
        functools.partial(_qkv_kernel, nb=nb, head_norm=head_norm),
        out_shape=(jax.ShapeDtypeStruct((t, p), F32),) * 3,
        grid=(t // tm, 3 * nb),
        in_specs=in_specs,
        out_specs=(pl.BlockSpec((tm, COL_TILE), part_map(0)),
                   pl.BlockSpec((tm, COL_TILE), part_map(nb)),
                   pl.BlockSpec((tm, COL_TILE), part_map(2 * nb))),
        scratch_shapes=[pltpu.VMEM((tm, d), BF16)],
        compiler_params=_params("arbitrary", "arbitrary"),
        name="qkv_proj",
    )(*args)


def _qkv_t_kernel(x_ref, g_ref, wt_ref, q_ref, kt_ref, vt_ref, xn_ref):
    j = pl.program_id(1)

    @pl.when(j == 0)
    def _():
        xn_ref[...] = _rms(x_ref[...], g_ref[...]).astype(BF16)
        q_ref[...] = (_dot_nt(xn_ref[...], wt_ref[...]) * Q_SCALE).astype(q_ref.dtype)

    @pl.when(j == 1)
    def _():
        kt_ref[...] = _dot_nt(wt_ref[...], xn_ref[...])

    @pl.when(j == 2)
    def _():
        vt_ref[...] = _dot_nt(wt_ref[...], xn_ref[...])


def _qkv_proj_t(x, g, wt_bf16, *, batch, seq, tm):
    t, d = x.shape
    p = wt_bf16.shape[0] // 3
    tiles_per_seq = seq // tm
    kv_map = lambda i, j: (i // tiles_per_seq, i % tiles_per_seq)
    return pl.pallas_call(
        _qkv_t_kernel,
        out_shape=(jax.ShapeDtypeStruct((t, p), BF16),
                   jax.ShapeDtypeStruct((batch * p, seq), F32),
                   jax.ShapeDtypeStruct((batch * p, seq), F32)),
        grid=(t // tm, 3),
        in_specs=[pl.BlockSpec((tm, d), lambda i, j: (i, 0)),
                  pl.BlockSpec((1, d), lambda i, j: (0, 0)),
                  pl.BlockSpec((p, d), lambda i, j: (j, 0))],
        out_specs=(pl.BlockSpec((tm, p), lambda i, j: (i, 0)),
                   pl.BlockSpec((p, tm), kv_map),
                   pl.BlockSpec((p, tm), kv_map)),
        scratch_shapes=[pltpu.VMEM((tm, d), BF16)],
        compiler_params=_params("arbitrary", "arbitrary"),
        name="qkv_proj_t",
    )(x, g.reshape(1, d), wt_bf16)


def _kv_state_kernel(x_ref, g_ref, wk_ref, wv_ref, gk_ref, bd_ref, o_ref):
    p = wk_ref.shape[0]
    xn = _rms(x_ref[...], g_ref[...]).astype(BF16)
    kt = _dot_nt(wk_ref[...], xn)
    for c in range(p // MXU_DIM):
        rows = slice(c * MXU_DIM, (c + 1) * MXU_DIM)
        kc = kt[rows]
        sq = kc * kc
        hi = sq.astype(BF16)
        lo = (sq - hi.astype(F32)).astype(BF16)
        ms = _dot(bd_ref[...], hi) + _dot(bd_ref[...], lo)
        o_ref[rows, :] = (kc * lax.rsqrt(ms + NORM_EPS)) * gk_ref[rows, :]
    o_ref[p:, :] = _dot_nt(wv_ref[...], xn)


def _kv_state_t(x, g, wt_bf16, gk, *, group, n_groups, batch, seq, keep):
    t, d = x.shape
    p = wt_bf16.shape[0] // (3 * n_groups)
    tm = min(keep, ROW_TILE)
    tiles = keep // tm
    first = (seq - keep) // tm
    per_seq = seq // tm
    head_of = jnp.arange(MXU_DIM) // HEAD_DIM
    bd = jnp.where(head_of[:, None] == head_of[None, :], 1.0 / HEAD_DIM, 0.0).astype(BF16)
    gk_col = jnp.tile(gk.astype(F32), p // HEAD_DIM).reshape(p, 1)
    return pl.pallas_call(
        _kv_state_kernel,
        out_shape=jax.ShapeDtypeStruct((batch * 2 * p, keep), F32),
        grid=(batch, tiles),
        in_specs=[pl.BlockSpec((tm, d), lambda b, i: (b * per_seq + first + i, 0)),
                  pl.BlockSpec((1, d), lambda b, i: (0, 0)),
                  pl.BlockSpec((p, d), lambda b, i: (n_groups + group, 0)),
                  pl.BlockSpec((p, d), lambda b, i: (2 * n_groups + group, 0)),
                  pl.BlockSpec((p, 1), lambda b, i: (0, 0)),
                  pl.BlockSpec((MXU_DIM, MXU_DIM), lambda b, i: (0, 0))],
        out_specs=pl.BlockSpec((2 * p, tm), lambda b, i: (b, i)),
        compiler_params=_params("arbitrary", "arbitrary"),
        name="dsw_kv_state",
    )(x, g.reshape(1, d), wt_bf16, wt_bf16, gk_col, bd)


def _out_proj_kernel(a_ref, w_ref, res_ref, o_ref):
    o_ref[...] = res_ref[...] + _dot(a_ref[...].astype(BF16), w_ref[...])


def _out_proj(a, w_bf16, res, *, tm):
    t, d_in = a.shape
    d = w_bf16.shape[1]
    return pl.pallas_call(
        _out_proj_kernel,
        out_shape=jax.ShapeDtypeStruct((t, d), F32),
        grid=(t // tm,),
        in_specs=[pl.BlockSpec((tm, d_in), lambda i: (i, 0)),
                  pl.BlockSpec((d_in, d), lambda i: (0, 0)),
                  pl.BlockSpec((tm, d), lambda i: (i, 0))],
        out_specs=pl.BlockSpec((tm, d), lambda i: (i, 0)),
        compiler_params=_params("arbitrary"),
        name="out_proj",
    )(a, w_bf16, res)


def _softplus_parts(z):
    sp = jnp.maximum(z, 0.0) + jnp.log(1.0 + jnp.exp(-jnp.abs(z)))
    return sp, z - sp


def _sb_prompt_kernel(bias_ref, q_ref, kt_ref, vt_ref, tri_ref, o_ref, kb_ref, vm_ref, acc_ref, *, tq, tk):
    hq = pl.program_id(1)
    qi = pl.program_id(2)
    nh = SB_HEADS_PER_BLOCK
    s_len = kt_ref.shape[1]
    head_of_lane = lax.broadcasted_iota(I32, (1, MXU_DIM), 1) // HEAD_DIM
    head_of_row = lax.broadcasted_iota(I32, (MXU_DIM, 1), 0) // HEAD_DIM

    @pl.when(qi == 0)
    def _():
        for j in range(s_len // tk):
            cols = slice(j * tk, (j + 1) * tk)
            kb_ref[j] = kt_ref[:, cols].astype(BF16)
            vj = vt_ref[:, cols]
            vm_ref[j] = jnp.concatenate(
                [jnp.where(head_of_row == h, vj, 0.0) for h in range(nh)], axis=1).astype(BF16)

    q = q_ref[...]
    qm = jnp.concatenate([jnp.where(head_of_lane == h, q, jnp.zeros_like(q)) for h in range(nh)], axis=0)
    bias = jnp.concatenate([jnp.full((tq, 1), bias_ref[hq * nh + h], F32) for h in range(nh)], axis=0)
    tri = tri_ref[...]
    acc_ref[...] = jnp.zeros_like(acc_ref)

    def step(j, carry, mask):
        z = _dot(qm, kb_ref[j]) + bias
        sp, ls = _softplus_parts(z)
        if mask is not None:
            sp = jnp.where(mask, sp, 0.0)
        after = _dot(sp.astype(BF16), tri) + carry
        w = jnp.exp(ls - after)
        if mask is not None:
            w = jnp.where(mask, w, 0.0)
        wb = w.astype(BF16)
        wcat = jnp.concatenate([wb[h * tq:(h + 1) * tq] for h in range(nh)], axis=1)
        acc_ref[...] += _dot_nt(wcat, vm_ref[j])
        return carry + jnp.sum(sp, axis=1, keepdims=True)

    assert tq == tk
    row = lax.broadcasted_iota(I32, (nh * tq, tk), 0) % tq
    col = lax.broadcasted_iota(I32, (nh * tq, tk), 1)
    carry = step(qi, jnp.zeros((nh * tq, 1), F32), col < row)
    lax.fori_loop(0, qi, lambda t, c: step(qi - 1 - t, c, None), carry)
    o_ref[...] = acc_ref[...].astype(o_ref.dtype)


def _sb_prompt_attention(q, kt, vt, bias, *, batch, seq):
    width = q.shape[1]
    blocks_per_batch = width // MXU_DIM
    tq, tk = SB_Q_TILE, SB_K_TILE
    nq = seq // tq
    idx = jnp.arange(tk)
    tri = (idx[:, None] > idx[None, :]).astype(BF16)
    return pl.pallas_call(
        functools.partial(_sb_prompt_kernel, tq=tq, tk=tk),
        out_shape=jax.ShapeDtypeStruct((batch * seq, width), BF16),
        grid=(batch, width // MXU_DIM, nq),
        in_specs=[pl.BlockSpec(memory_space=pltpu.SMEM),
                  pl.BlockSpec((tq, MXU_DIM), lambda b, h, i: (b * nq + i, h)),
                  pl.BlockSpec((MXU_DIM, seq), lambda b, h, i: (b * blocks_per_batch + h, 0)),
                  pl.BlockSpec((MXU_DIM, seq), lambda b, h, i: (b * blocks_per_batch + h, 0)),
                  pl.BlockSpec((tk, tk), lambda b, h, i: (0, 0))],
        out_specs=pl.BlockSpec((tq, MXU_DIM), lambda b, h, i: (b * nq + i, h)),
        scratch_shapes=[pltpu.VMEM((seq // tk, MXU_DIM, tk), BF16),
                        pltpu.VMEM((seq // tk, MXU_DIM, SB_HEADS_PER_BLOCK * tk), BF16),
                        pltpu.VMEM((tq, MXU_DIM), F32)],
        compiler_params=_params("arbitrary", "arbitrary", "arbitrary"),
        name="sb_prompt_attention",
    )(bias.astype(F32), q, kt, vt, tri)


def _head_rows(x_row, n_heads):
    width = x_row.shape[1]
    lane_head = lax.broadcasted_iota(I32, (n_heads, width), 1) // HEAD_DIM
    row = lax.broadcasted_iota(I32, (n_heads, width), 0)
    return jnp.where(lane_head == row, jnp.broadcast_to(x_row, (n_heads, width)), 0.0)


def _head_diag(acc):
    n_heads, width = acc.shape
    lane_head = lax.broadcasted_iota(I32, (n_heads, width), 1) // HEAD_DIM
    row = lax.broadcasted_iota(I32, (n_heads, width), 0)
    return jnp.sum(jnp.where(lane_head == row, acc, 0.0), axis=0, keepdims=True)


def _sb_sample_kernel(pt_ref, q_ref, bias_ref, tri_ref, *refs, n_pages_step, n_heads):
    del pt_ref
    k_refs = refs[:n_pages_step]
    v_refs = refs[n_pages_step:2 * n_pages_step]
    o_ref, acc_ref, carry_ref = refs[2 * n_pages_step:]
    s = pl.program_id(1)

    @pl.when(s == 0)
    def _():
        acc_ref[...] = jnp.zeros_like(acc_ref)
        carry_ref[...] = jnp.zeros_like(carry_ref)

    qh = _head_rows(q_ref[0], n_heads).astype(BF16)
    kcat = jnp.concatenate([r[...].astype(BF16) for r in k_refs], axis=1)
    vcat = jnp.concatenate([r[...].astype(BF16) for r in v_refs], axis=1)
    z = _dot(qh, kcat) + bias_ref[...]
    sp, ls = _softplus_parts(z)
    sub = tri_ref.shape[0]
    carry = carry_ref[...]
    afters = []
    for c in reversed(range(z.shape[1] // sub)):
        sp_c = sp[:, c * sub:(c + 1) * sub]
        sp_hi = sp_c.astype(BF16)
        sp_lo = (sp_c - sp_hi.astype(F32)).astype(BF16)
        afters.append(_dot(sp_hi, tri_ref[...]) + _dot(sp_lo, tri_ref[...]) + carry)
        carry = carry + jnp.sum(sp_c, axis=1, keepdims=True)
    after = jnp.concatenate(afters[::-1], axis=1)
    w = jnp.exp(ls - after)
    acc_ref[...] += _dot_nt(w.astype(BF16), vcat)
    carry_ref[...] = carry

    @pl.when(s == pl.num_programs(1) - 1)
    def _():
        o_ref[0] = _head_diag(acc_ref[...])


def _sb_sample_attention(q, cache_k, cache_v, page_table, bias):
    bd, width = q.shape
    n_heads = width // HEAD_DIM
    page = cache_k.shape[2]
    n_pages = page_table.shape[1]
    pps = SB_PAGES_PER_STEP
    n_steps = n_pages // pps
    idx = jnp.arange(SB_SAMPLE_SUB_BLOCK)
    tri = (idx[:, None] > idx[None, :]).astype(BF16)
    assert (pps * page) % SB_SAMPLE_SUB_BLOCK == 0

    def page_map(u):
        return lambda b, s, pt: (pt[b * n_pages + (n_steps - 1 - s) * pps + u], 0, 0)

    kv_specs = [pl.BlockSpec((None, width, page), page_map(u)) for u in range(pps)]
    out = pl.pallas_call(
        functools.partial(_sb_sample_kernel, n_pages_step=pps, n_heads=n_heads),
        out_shape=jax.ShapeDtypeStruct((bd, 1, width), F32),
        grid_spec=pltpu.PrefetchScalarGridSpec(
            num_scalar_prefetch=1,
            grid=(bd, n_steps),
            in_specs=[pl.BlockSpec((1, 1, width), lambda b, s, pt: (b, 0, 0)),
                      pl.BlockSpec((n_heads, 1), lambda b, s, pt: (0, 0)),
                      pl.BlockSpec((SB_SAMPLE_SUB_BLOCK, SB_SAMPLE_SUB_BLOCK), lambda b, s, pt: (0, 0))]
            + kv_specs + kv_specs,
            out_specs=pl.BlockSpec((1, 1, width), lambda b, s, pt: (b, 0, 0)),
            scratch_shapes=[pltpu.VMEM((n_heads, width), F32), pltpu.VMEM((n_heads, 1), F32)],
        ),
        compiler_params=_params("arbitrary", "arbitrary"),
        name="sb_sample_attention",
    )(page_table.reshape(-1).astype(I32), q.astype(F32).reshape(bd, 1, width),
      bias.astype(F32).reshape(n_heads, 1), tri, *([cache_k] * pps), *([cache_v] * pps))
    return out.reshape(bd, width)


def _dsw_prompt_kernel(slope_ref, *refs, seq):
    q_refs, k_refs, v_refs = refs[0:3], refs[3:6], refs[6:9]
    o_ref, og_ref, lg_ref = refs[9:]
    hp = pl.program_id(1)
    blk_rows = DSW_BLOCK
    n_groups = len(DSW_DILATIONS)
    heads_per_group = slope_ref.shape[0] // n_groups
    lane = lax.broadcasted_iota(I32, (1, LANES), 1)
    first_head = lane < HEAD_DIM

    qrow = lax.broadcasted_iota(I32, (2 * blk_rows, 2 * blk_rows), 0) % blk_rows
    kcol = lax.broadcasted_iota(I32, (2 * blk_rows, 2 * blk_rows), 1)
    steps = qrow + blk_rows - kcol
    in_band = (steps >= 0) & (steps <= DSW_SPAN)
    steps_f = steps.astype(F32)
    upper = lax.broadcasted_iota(I32, (2 * blk_rows, 1), 0) >= blk_rows

    for g, dil in enumerate(DSW_DILATIONS):
        n_blk = seq // dil // blk_rows
        s0 = slope_ref[g * heads_per_group + 2 * hp]
        s1 = slope_ref[g * heads_per_group + 2 * hp + 1]
        penalty = jnp.where(upper, s1, s0) * steps_f
        q_ref, k_ref, v_ref = q_refs[g], k_refs[g], v_refs[g]

        def rows_at(first, dil=dil):
            if dil == 1:
                return pl.ds(pl.multiple_of(first, blk_rows), blk_rows)
            return pl.ds(first, blk_rows, stride=dil)

        def block(t, _, g=g, dil=dil, n_blk=n_blk, penalty=penalty, q_ref=q_ref, k_ref=k_ref,
                  v_ref=v_ref, rows_at=rows_at):
            res = t // n_blk
            blk = t % n_blk
            own = res + dil * blk * blk_rows
            prev = res + dil * jnp.maximum(blk - 1, 0) * blk_rows
            qb = q_ref[rows_at(own), :]
            kc = jnp.concatenate([k_ref[rows_at(prev), :], k_ref[rows_at(own), :]], axis=0)
            vc = jnp.concatenate([v_ref[rows_at(prev), :], v_ref[rows_at(own), :]], axis=0)
            qm = jnp.concatenate([jnp.where(first_head, qb, 0.0), jnp.where(first_head, 0.0, qb)], axis=0)
            sc = _dot_nt(qm.astype(BF16), kc.astype(BF16)) - penalty
            valid = in_band & ((kcol >= blk_rows) | (blk > 0))
            sc = jnp.where(valid, sc, NEG_BIG)
            m = jnp.max(sc, axis=1, keepdims=True)
            p = jnp.exp(sc - m)
            l = jnp.sum(p, axis=1, keepdims=True)
            pb = p.astype(BF16)
            pcat = jnp.concatenate([pb[:blk_rows], pb[blk_rows:]], axis=1)
            vm = jnp.concatenate([jnp.where(first_head, vc, 0.0), jnp.where(first_head, 0.0, vc)], axis=0)
            o = _dot(pcat, vm.astype(BF16))
            lse = m + jnp.log(l)
            l_b = jnp.where(first_head, l[:blk_rows], l[blk_rows:])
            lse_b = jnp.where(first_head, lse[:blk_rows], lse[blk_rows:])
            og_ref[g, rows_at(own), :] = o / l_b
            lg_ref[g, rows_at(own), :] = lse_b
            return 0

        n_iter = dil * n_blk // DSW_BLOCKS_PER_ITER
        assert n_iter * DSW_BLOCKS_PER_ITER == dil * n_blk

        def blocks(it, carry, block=block):
            for u in range(DSW_BLOCKS_PER_ITER):
                block(it * DSW_BLOCKS_PER_ITER + u, carry)
            return carry

        lax.fori_loop(0, n_iter, blocks, 0)

    chunk = 2 * blk_rows
    for c in range(seq // chunk):
        rows = pl.ds(c * chunk, chunk)
        ls = [lg_ref[g, rows, :] for g in range(n_groups)]
        m = functools.reduce(jnp.maximum, ls)
        es = [jnp.exp(x - m) for x in ls]
        num = sum(e * og_ref[g, rows, :] for g, e in enumerate(es))
        o_ref[rows, :] = (num / sum(es)).astype(o_ref.dtype)


def _dsw_prompt_attention(q, k, v, slopes, *, batch, seq):
    n_groups = len(DSW_DILATIONS)
    width = q.shape[1] // n_groups
    blocks_per_group = width // LANES

    def col_map(g):
        return lambda b, h: (b, g * blocks_per_group + h)

    specs = [pl.BlockSpec((seq, LANES), col_map(g)) for g in range(n_groups)]
    return pl.pallas_call(
        functools.partial(_dsw_prompt_kernel, seq=seq),
        out_shape=jax.ShapeDtypeStruct((batch * seq, width), BF16),
        grid=(batch, blocks_per_group),
        in_specs=[pl.BlockSpec(memory_space=pltpu.SMEM)] + specs * 3,
        out_specs=pl.BlockSpec((seq, LANES), lambda b, h: (b, h)),
        scratch_shapes=[pltpu.VMEM((n_groups, seq, LANES), F32), pltpu.VMEM((n_groups, seq, LANES), F32)],
        compiler_params=_params("arbitrary", "arbitrary"),
        name="dsw_prompt_attention",
    )(slopes, *([q] * n_groups), *([k] * n_groups), *([v] * n_groups))


def _dsw_sample_kernel(slope_ref, q_ref, kn_ref, vn_ref, *refs, rows, heads_per_block):
    n_groups = len(rows)
    buf_refs = refs[:n_groups]
    o_ref = refs[n_groups]
    out_refs = refs[n_groups + 1:2 * n_groups + 1]
    p_refs = refs[2 * n_groups + 1:3 * n_groups + 1]
    pn_ref, alpha_ref = refs[3 * n_groups + 1:]
    j = pl.program_id(1)
    nq = pl.num_programs(1) // 2
    hb = heads_per_block
    width = hb * HEAD_DIM

    def column(row):
        eye = lax.broadcasted_iota(I32, (width, width), 0) == lax.broadcasted_iota(I32, (width, width), 1)
        return jnp.sum(jnp.where(eye, jnp.broadcast_to(row, (width, width)), 0.0), axis=1, keepdims=True)

    def moved_down(x, new_row):
        n = x.shape[1]
        lane = lax.broadcasted_iota(I32, (1, n), 1)
        return jnp.where(lane == n - 1, column(new_row), pltpu.roll(x, n - 1, 1))

    def rounded(x):
        return x.astype(BF16).astype(F32)

    @pl.when(j < nq)
    def _():
        lses, pns = [], []
        for g, dil in enumerate(DSW_DILATIONS):
            kt = buf_refs[g][...]
            qh = _head_rows(q_ref[0, pl.ds(g * nq + j, 1), :], hb)
            k_new = kn_ref[0, pl.ds(g * nq + j, 1), :]
            row = lax.broadcasted_iota(I32, (1, rows[g]), 1)
            back = (rows[g] - row).astype(F32)
            sc = _dot(qh.astype(BF16), kt.astype(BF16)) - slope_ref[g] * back
            sc = jnp.where((row & (dil - 1)) == 0, sc, NEG_BIG)
            sn = jnp.sum(rounded(qh) * rounded(k_new), axis=1, keepdims=True)
            m = jnp.maximum(jnp.max(sc, axis=1, keepdims=True), sn)
            lse = m + jnp.log(jnp.sum(jnp.exp(sc - m), axis=1, keepdims=True) + jnp.exp(sn - m))
            p_refs[g][j] = jnp.exp(sc - lse)
            pns.append(jnp.exp(sn - lse))
            lses.append(lse)
            out_refs[g][...] = moved_down(kt, k_new)
        m = functools.reduce(jnp.maximum, lses)
        es = [jnp.exp(l - m) for l in lses]
        pn_ref[j] = jnp.concatenate(pns, axis=1)
        alpha_ref[j] = jnp.concatenate(es, axis=1) / sum(es)

    @pl.when(j >= nq)
    def _():
        jq = j - nq
        pn = rounded(pn_ref[jq])
        alpha = rounded(alpha_ref[jq])
        acc = jnp.zeros((hb, width), F32)
        for g in range(n_groups):
            vt = buf_refs[g][...]
            v_new = vn_ref[0, pl.ds(g * nq + jq, 1), :]
            o_g = _dot_nt(p_refs[g][jq].astype(BF16), vt.astype(BF16)) + pn[:, g:g + 1] * rounded(v_new)
            acc = acc + alpha[:, g:g + 1] * rounded(o_g)
            out_refs[g][...] = moved_down(vt, v_new)
        o_ref[0] = _head_diag(acc)


def _dsw_sample_attention(q, k_new, v_new, bufs_t, slopes):
    bd, total = q.shape
    n_groups = len(DSW_DILATIONS)
    width = total // n_groups
    rows = tuple(b.shape[2] for b in bufs_t)
    for g, dil in enumerate(DSW_DILATIONS):
        assert rows[g] == DSW_SPAN * dil
    hb = DSW_SAMPLE_HEADS
    block = hb * HEAD_DIM
    nq = width // block
    blocked = lambda a: a.astype(F32).reshape(bd, n_groups * nq, block)
    vec_spec = pl.BlockSpec((1, n_groups * nq, block), lambda b, j: (b, 0, 0))
    buf_specs = [pl.BlockSpec((None, block, r), lambda b, j: (b, j, 0)) for r in rows]
    outs = pl.pallas_call(
        functools.partial(_dsw_sample_kernel, rows=rows, heads_per_block=hb),
        out_shape=(jax.ShapeDtypeStruct((bd * nq, 1, block), F32),)
        + tuple(jax.ShapeDtypeStruct(b.shape, F32) for b in bufs_t),
        grid=(bd, 2 * nq),
        in_specs=[pl.BlockSpec((n_groups, None, hb, 1), lambda b, j: (0, jnp.minimum(j, nq - 1), 0, 0)),
                  vec_spec, vec_spec, vec_spec] + buf_specs,
        out_specs=(pl.BlockSpec((1, 1, block), lambda b, j: (b * nq + jnp.maximum(j - nq, 0), 0, 0)),)
        + tuple(buf_specs),
        scratch_shapes=[pltpu.VMEM((nq, hb, r), F32) for r in rows]
        + [pltpu.VMEM((nq, hb, n_groups), F32), pltpu.VMEM((nq, hb, n_groups), F32)],
        compiler_params=_params("arbitrary", "arbitrary"),
        name="dsw_sample_attention",
    )(slopes.reshape(n_groups, nq, hb, 1), blocked(q), blocked(k_new), blocked(v_new), *bufs_t)
    return outs[0].reshape(bd, width), outs[1:]


def _router_kernel(hp_ref, hs_ref, g_ref, w_ref, b_ref, xn_ref, eid_ref, gate_ref, *, n_groups, n_experts):
    i = pl.program_id(0)
    n_prompt_tiles = pl.num_programs(0) - 1
    lane = lax.broadcasted_iota(I32, (1, LANES), 1)
    lane_f = lane.astype(F32)
    big = float(LANES)

    def first_lane(hit):
        return jnp.min(jnp.where(hit, lane_f, big), axis=1, keepdims=True).astype(I32)

    def route(x):
        xn = _rms(x, g_ref[...])
        logits = _dot(xn.astype(BF16), w_ref[...]) + b_ref[...]
        gl = jnp.where(lane < n_groups, logits, -jnp.inf)
        gmax = jnp.max(gl, axis=1, keepdims=True)
        gsel = first_lane(gl == gmax)
        gden = jnp.sum(jnp.exp(gl - gmax), axis=1, keepdims=True)
        first = n_groups + gsel * n_experts
        el = jnp.where((lane >= first) & (lane < first + n_experts), logits, -jnp.inf)
        m1 = jnp.max(el, axis=1, keepdims=True)
        i1 = first_lane(el == m1)
        el2 = jnp.where(lane == i1, -jnp.inf, el)
        m2 = jnp.max(el2, axis=1, keepdims=True)
        i2 = first_lane(el2 == m2)
        e2 = jnp.exp(m2 - m1)
        w1 = 1.0 / (1.0 + e2) / gden
        w2 = e2 / (1.0 + e2) / gden
        eid = jnp.where(lane == 0, i1 - n_groups, jnp.where(lane == 1, i2 - n_groups, 0))
        gate = jnp.where(lane == 0, w1, jnp.where(lane == 1, w2, 0.0))
        return xn, eid, gate

    def store_token_major(xn):
        for s in range(xn.shape[1] // LANES):
            xn_ref[pl.ds(s, xn.shape[0], stride=TOKEN_ROWS), :] = xn[:, s * LANES:(s + 1) * LANES]

    @pl.when(i < n_prompt_tiles)
    def _():
        xn, eid, gate = route(hp_ref[...])
        store_token_major(xn)
        eid_ref[...] = eid
        gate_ref[...] = gate

    @pl.when(i == n_prompt_tiles)
    def _():
        ns = hs_ref.shape[0]
        xn, eid, gate = route(hs_ref[...])
        xn_ref[...] = jnp.zeros_like(xn_ref)
        eid_ref[...] = jnp.zeros_like(eid_ref)
        gate_ref[...] = jnp.zeros_like(gate_ref)
        store_token_major(xn)
        eid_ref[0:ns, :] = eid
        gate_ref[0:ns, :] = gate


def _router(h_prompt, h_sample, g, w_router, b_router, *, n_groups, n_experts, tm):
    tp, d = h_prompt.shape
    assert d == TOKEN_ROWS * LANES
    ns = h_sample.shape[0]
    n_tiles = tp // tm + 1
    last = tp // tm - 1
    return pl.pallas_call(
        functools.partial(_router_kernel, n_groups=n_groups, n_experts=n_experts),
        out_shape=(jax.ShapeDtypeStruct((n_tiles * tm * TOKEN_ROWS, LANES), F32),
                   jax.ShapeDtypeStruct((n_tiles * tm, LANES), I32),
                   jax.ShapeDtypeStruct((n_tiles * tm, LANES), F32)),
        grid=(n_tiles,),
        in_specs=[pl.BlockSpec((tm, d), lambda i: (jnp.minimum(i, last), 0)),
                  pl.BlockSpec((ns, d), lambda i: (0, 0)),
                  pl.BlockSpec((1, d), lambda i: (0, 0)),
                  pl.BlockSpec((d, LANES), lambda i: (0, 0)),
                  pl.BlockSpec((1, LANES), lambda i: (0, 0))],
        out_specs=(pl.BlockSpec((tm * TOKEN_ROWS, LANES), lambda i: (i, 0)),
                   pl.BlockSpec((tm, LANES), lambda i: (i, 0)),
                   pl.BlockSpec((tm, LANES), lambda i: (i, 0))),
        compiler_params=_params("arbitrary"),
        name="moe_router",
    )(h_prompt, h_sample, g.reshape(1, d), w_router, b_router)


def _moe_kernel(tile_e_ref, nvalid_ref, rows_ref, slot_ref, slot_next_ref, gate_ref, x_hbm,
                wg_ref, wu_ref, wd_ref, y_hbm, xbuf, ybuf, wgb, wub, wdb, gsem, ssem):
    i = pl.program_id(0)
    nv = nvalid_ref[0]
    tm = xbuf.shape[1] // TOKEN_ROWS
    n_chunks = TOKEN_ROWS
    slot = lax.rem(i, 2)

    def tile_rows(first):
        return pl.ds(pl.multiple_of(first * TOKEN_ROWS, TOKEN_ROWS), TOKEN_ROWS)

    def gather(idx_ref, s):
        def issue(r, _):
            pltpu.make_async_copy(x_hbm.at[tile_rows(idx_ref[0, 0, r] >> 1), :], xbuf.at[s, tile_rows(r), :],
                                  gsem.at[s]).start()
            return 0
        lax.fori_loop(0, tm, issue, 0, unroll=8)

    def gather_done(s):
        return pltpu.make_async_copy(x_hbm.at[pl.ds(0, tm * TOKEN_ROWS), :], xbuf.at[s], gsem.at[s])

    def scatter_wait(s, tile):
        n = pl.multiple_of(rows_ref[tile] * TOKEN_ROWS, TOKEN_ROWS)
        pltpu.make_async_copy(ybuf.at[s, pl.ds(0, n), :], y_hbm.at[pl.ds(0, n), :], ssem.at[s]).wait()

    @pl.when(i == 0)
    def _():
        gather(slot_ref, 0)

    @pl.when(i < nv)
    def _():
        gather_done(slot).wait()

        @pl.when(i + 1 < nv)
        def _():
            gather(slot_next_ref, 1 - slot)

        @pl.when((i == 0) | (tile_e_ref[i] != tile_e_ref[jnp.maximum(i - 1, 0)]))
        def _():
            wgb[...] = wg_ref[...].astype(BF16)
            wub[...] = wu_ref[...].astype(BF16)
            wdb[...] = wd_ref[...].astype(BF16)

        x = jnp.concatenate([xbuf[slot, pl.ds(c, tm, stride=TOKEN_ROWS), :] for c in range(n_chunks)],
                            axis=1).astype(BF16)
        hg = _dot(x, wgb[...])
        hu = _dot(x, wub[...])
        h = (hg / (1.0 + jnp.exp(-hg))) * hu * gate_ref[...]
        y = _dot(h.astype(BF16), wdb[...])

        @pl.when(i >= 2)
        def _():
            scatter_wait(slot, i - 2)

        for c in range(n_chunks):
            ybuf[slot, pl.ds(c, tm, stride=TOKEN_ROWS), :] = y[:, c * LANES:(c + 1) * LANES]
        n_rows = rows_ref[i]

        def issue(pair, _):
            for queue in range(MOE_DMA_QUEUES):
                r = pair * MOE_DMA_QUEUES + queue

                @pl.when(r < n_rows)
                def _(r=r, queue=queue):
                    pltpu.make_async_copy(ybuf.at[slot, tile_rows(r), :], y_hbm.at[tile_rows(slot_ref[0, 0, r]), :],
                                          ssem.at[slot]).start(priority=queue)
            return 0
        lax.fori_loop(0, tm // MOE_DMA_QUEUES, issue, 0, unroll=4)

        @pl.when(i == nv - 1)
        def _():
            scatter_wait(slot, i)

            @pl.when(i >= 1)
            def _():
                scatter_wait(1 - slot, i - 1)


def _moe_experts(xn_all, eid, gate, w_gate, w_up, w_down, *, n_real, first_expert, n_exp):
    assert MOE_TOP_K == 2
    _, d, d_ff = w_gate.shape
    tm = MOE_TILE
    n_slots = n_real * MOE_TOP_K
    n_tiles = -(-n_slots // tm) + n_exp
    n_pos = n_tiles * tm

    e_flat = eid.reshape(-1)
    tok_flat = jnp.repeat(jnp.arange(n_real, dtype=I32), MOE_TOP_K)
    k_flat = jnp.tile(jnp.arange(MOE_TOP_K, dtype=I32), n_real)
    onehot = (e_flat[:, None] == jnp.arange(n_exp, dtype=I32)[None, :]).astype(I32)
    csum = jnp.cumsum(onehot, axis=0)
    rank = jnp.sum((csum - 1) * onehot, axis=1)
    counts = csum[-1]
    tiles_e = (counts + tm - 1) // tm
    tiles_end = jnp.cumsum(tiles_e)
    tile_start = tiles_end - tiles_e
    pos = jnp.sum(onehot * tile_start[None, :], axis=1) * tm + rank
    n_valid = tiles_end[-1]
    tile_ids = jnp.arange(n_tiles, dtype=I32)
    tile_e = jnp.minimum(jnp.sum((tiles_end[None, :] <= tile_ids[:, None]).astype(I32), axis=1), n_exp - 1)
    tile_first = jnp.sum((tile_e[:, None] == jnp.arange(n_exp, dtype=I32)[None, :]) * tile_start[None, :], axis=1)
    tile_count = jnp.sum((tile_e[:, None] == jnp.arange(n_exp, dtype=I32)[None, :]) * counts[None, :], axis=1)
    tile_rows = jnp.where(tile_ids < n_valid, jnp.clip(tile_count - (tile_ids - tile_first) * tm, 0, tm), 0).astype(I32)
    tile_e = jnp.where(tile_ids < n_valid, tile_e, tile_e[jnp.maximum(n_valid - 1, 0)]).astype(I32) + first_expert
    slot_flat = tok_flat * MOE_TOP_K + k_flat
    packed = jnp.stack([slot_flat, lax.bitcast_convert_type(gate.reshape(-1).astype(F32), I32)], axis=1)
    table = jnp.zeros((n_pos, 2), I32).at[pos].set(packed, unique_indices=True)
    slot3 = table[:, 0].reshape(n_tiles, 1, tm)
    gate_tbl = lax.bitcast_convert_type(table[:, 1], F32).reshape(n_pos, 1)

    smem_tile = lambda off: pl.BlockSpec(
        (1, 1, tm), lambda i, te, nv, tr: (jnp.minimum(i + off, n_tiles - 1), 0, 0), memory_space=pltpu.SMEM)
    expert_block = lambda shape: pl.BlockSpec((None,) + shape, lambda i, te, nv, tr: (te[i], 0, 0))
    return pl.pallas_call(
        _moe_kernel,
        out_shape=jax.ShapeDtypeStruct((n_slots * TOKEN_ROWS, LANES), F32),
        grid_spec=pltpu.PrefetchScalarGridSpec(
            num_scalar_prefetch=3,
            grid=(n_tiles,),
            in_specs=[smem_tile(0), smem_tile(1),
                      pl.BlockSpec((tm, 1), lambda i, te, nv, tr: (i, 0)),
                      pl.BlockSpec(memory_space=pl.ANY),
                      expert_block((d, d_ff)), expert_block((d, d_ff)), expert_block((d_ff, d))],
            out_specs=pl.BlockSpec(memory_space=pl.ANY),
            scratch_shapes=[pltpu.VMEM((2, tm * TOKEN_ROWS, LANES), F32), pltpu.VMEM((2, tm * TOKEN_ROWS, LANES), F32),
                            pltpu.VMEM((d, d_ff), BF16), pltpu.VMEM((d, d_ff), BF16), pltpu.VMEM((d_ff, d), BF16),
                            pltpu.SemaphoreType.DMA((2,)), pltpu.SemaphoreType.DMA((2,))],
        ),
        compiler_params=_params("arbitrary"),
        name="moe_experts",
    )(tile_e, n_valid.reshape(1).astype(I32), tile_rows, slot3, slot3, gate_tbl, xn_all, w_gate, w_up, w_down)


def _combine_kernel(h_ref, y_ref, o_ref):
    tm = h_ref.shape[0]
    stride = MOE_TOP_K * TOKEN_ROWS
    for c in range(TOKEN_ROWS):
        cols = slice(c * LANES, (c + 1) * LANES)
        o_ref[:, cols] = h_ref[:, cols] + (y_ref[pl.ds(c, tm, stride=stride), :]
                                           + y_ref[pl.ds(TOKEN_ROWS + c, tm, stride=stride), :])


def _combine(h, y_tiles, *, first_row, tm):
    t, d = h.shape
    b0 = first_row // tm
    rows = tm * MOE_TOP_K * TOKEN_ROWS
    return pl.pallas_call(
        _combine_kernel,
        out_shape=jax.ShapeDtypeStruct((t, d), F32),
        grid=(t // tm,),
        in_specs=[pl.BlockSpec((tm, d), lambda i: (i, 0)),
                  pl.BlockSpec((rows, LANES), lambda i: (b0 + i, 0))],
        out_specs=pl.BlockSpec((tm, d), lambda i: (i, 0)),
        compiler_params=_params("arbitrary"),
        name="moe_combine",
    )(h, y_tiles)


def _hier_moe(hp, hs, g, w_group, b_group, w_expert, b_expert, w_gate, w_up, w_down, layer):
    tp, d = hp.shape
    ns = hs.shape[0]
    n_groups, _, n_experts = w_expert.shape
    n_exp = n_groups * n_experts
    n_route = n_groups * (1 + n_experts)
    w_router = jnp.concatenate(
        [w_group, jnp.moveaxis(w_expert, 0, 1).reshape(d, n_groups * n_experts),
         jnp.zeros((d, LANES - n_route), F32)], axis=1).astype(BF16)
    b_router = jnp.concatenate(
        [b_group, b_expert.reshape(-1), jnp.zeros((LANES - n_route,), F32)]).astype(F32).reshape(1, LANES)
    xn_all, eid, gate = _router(hp, hs, g, w_router, b_router, n_groups=n_groups, n_experts=n_experts, tm=ROW_TILE)
    n_real = tp + ns
    d_ff = w_gate.shape[-1]
    y_tiles = _moe_experts(xn_all, eid[:n_real, :MOE_TOP_K], gate[:n_real, :MOE_TOP_K],
                           w_gate.reshape(-1, d, d_ff), w_up.reshape(-1, d, d_ff), w_down.reshape(-1, d_ff, d),
                           n_real=n_real, first_expert=layer * n_exp, n_exp=n_exp)
    return (_combine(hp, y_tiles, first_row=0, tm=ROW_TILE),
            _combine(hs, y_tiles, first_row=tp, tm=ns))


def kernel(x_prompt, x_sample, cache_sb_k, cache_sb_v, cache_dsw0_kv, cache_dsw1_kv, cache_dsw2_kv, page_table, norm1_g, norm2_g, sb_w_in, sb_logit_bias, sb_w_out, dsw_w_in, dsw_q_norm_g, dsw_k_norm_g, dsw_w_out, moe_w_group, moe_b_group, moe_w_expert, moe_b_expert, moe_w_gate, moe_w_up, moe_w_down):
    batch, seq, d = x_prompt.shape
    bd, dec_seq, _ = x_sample.shape
    assert dec_seq == 1
    depth = norm1_g.shape[0]
    dsw_caches = (cache_dsw0_kv, cache_dsw1_kv, cache_dsw2_kv)
    n_groups = len(DSW_DILATIONS)
    sb_heads = cache_sb_k.shape[3]
    dsw_heads = cache_dsw0_kv.shape[4]
    page = cache_sb_k.shape[2]
    tp = batch * seq

    hp = x_prompt.reshape(tp, d)
    hs = x_sample.reshape(bd, d)
    sb_kp, sb_vp, sb_ks, sb_vs = [], [], [], []
    dsw_p = [[] for _ in range(n_groups)]
    dsw_s = [[] for _ in range(n_groups)]

    for layer in range(depth):
        if layer % 2 == 0:
            la = layer // 2
            w_in = sb_w_in[la].astype(BF16)
            w_out = sb_w_out[la].astype(BF16)
            width = sb_heads * HEAD_DIM
            qp, kpt, vpt = _qkv_proj_t(hp, norm1_g[layer], w_in.T, batch=batch, seq=seq, tm=SB_QKV_ROW_TILE)
            qs, ks, vs = _qkv_proj(hs, norm1_g[layer], w_in, tm=bd)
            op = _sb_prompt_attention(qp, kpt, vpt, sb_logit_bias[la], batch=batch, seq=seq)
            pages_t = lambda c: jnp.transpose(c[:, la], (0, 2, 3, 1)).reshape(-1, width, page)
            os_ = _sb_sample_attention(qs, pages_t(cache_sb_k), pages_t(cache_sb_v), page_table, sb_logit_bias[la])
            hp = _out_proj(op, w_out, hp, tm=ROW_TILE)
            hs = _out_proj(os_, w_out, hs, tm=bd)
            seq_major = lambda a: jnp.transpose(a.reshape(batch, sb_heads, HEAD_DIM, seq), (0, 3, 1, 2))
            sb_kp.append(seq_major(kpt))
            sb_vp.append(seq_major(vpt))
            sb_ks.append(ks.reshape(bd, 1, sb_heads, HEAD_DIM))
            sb_vs.append(vs.reshape(bd, 1, sb_heads, HEAD_DIM))
        else:
            lb = layer // 2
            w_in = dsw_w_in[lb].astype(BF16)
            w_out = dsw_w_out[lb].astype(BF16)
            gains = (dsw_q_norm_g[lb], dsw_k_norm_g[lb])
            width = dsw_heads * HEAD_DIM
            n = jnp.arange(1, n_groups * dsw_heads + 1, dtype=F32)
            slopes = (2.0 ** (-8.0 * n / (n_groups * dsw_heads))).reshape(n_groups, dsw_heads)
            step_penalty = slopes * jnp.asarray(DSW_DILATIONS, F32)[:, None]
            qp, kp, vp = _qkv_proj(hp, norm1_g[layer], w_in, tm=DSW_QKV_ROW_TILE, head_gains=gains)
            qs, ks, vs = _qkv_proj(hs, norm1_g[layer], w_in, tm=bd, head_gains=gains)
            op = _dsw_prompt_attention(qp, kp, vp, step_penalty.reshape(-1), batch=batch, seq=seq)
            bufs_t = [jnp.transpose(c[lb], (0, 2, 3, 4, 1)).reshape(bd, 2 * width, c.shape[2]) for c in dsw_caches]
            os_, new_bufs_t = _dsw_sample_attention(qs, ks, vs, bufs_t, slopes)
            w_in_t = w_in.T
            for g in range(n_groups):
                keep = min(dsw_caches[g].shape[2], seq)
                state_t = _kv_state_t(hp, norm1_g[layer], w_in_t, gains[1], group=g, n_groups=n_groups,
                                      batch=batch, seq=seq, keep=keep)
                dsw_p[g].append(jnp.transpose(state_t.reshape(batch, 2, dsw_heads, HEAD_DIM, keep), (0, 4, 1, 2, 3)))
            hp = _out_proj(op, w_out, hp, tm=ROW_TILE)
            hs = _out_proj(os_, w_out, hs, tm=bd)
            for g in range(n_groups):
                rows_g = new_bufs_t[g].shape[2]
                dsw_s[g].append(jnp.transpose(new_bufs_t[g].reshape(bd, 2, dsw_heads, HEAD_DIM, rows_g),
                                              (0, 4, 1, 2, 3)))
        hp, hs = _hier_moe(hp, hs, norm2_g[layer], moe_w_group[layer], moe_b_group[layer], moe_w_expert[layer],
                           moe_b_expert[layer], moe_w_gate, moe_w_up, moe_w_down, layer)

    return (hp.reshape(batch, seq, d), hs.reshape(bd, 1, d),
            jnp.stack(sb_kp, axis=1), jnp.stack(sb_vp, axis=1),
            jnp.stack(sb_ks, axis=1), jnp.stack(sb_vs, axis=1),
            jnp.stack(dsw_p[0], axis=0), jnp.stack(dsw_s[0], axis=0),
            jnp.stack(dsw_p[1], axis=0), jnp.stack(dsw_s[1], axis=0),
            jnp.stack(dsw_p[2], axis=0), jnp.stack(dsw_s[2], axis=0))
```
